```python
import math
import jax, jax.numpy as jnp
from jax import lax
import numpy as np

D_MODEL = 1024
BATCH = 16
SEQ = 2048
DEPTH = 1

CTX_LEN = 256
GRID_W = 64

SSD_HEADS = 16
SSD_HEAD_DIM = 64
D_SSD = SSD_HEADS * SSD_HEAD_DIM
SSD_GROUPS = 2
SSD_HEADS_PER_GROUP = SSD_HEADS // SSD_GROUPS
D_STATE = 128
CONV_W = 5
CHUNK = 128
D_XBC = D_SSD + 2 * SSD_GROUPS * D_STATE

ATTN_Q_HEADS = 16
ATTN_KV_HEADS = 4
HEAD_DIM = 64
GQA_GROUP = ATTN_Q_HEADS // ATTN_KV_HEADS
D_ATTN = ATTN_Q_HEADS * HEAD_DIM
ROPE_THETA = 10000.0
Q_BLOCK = 128

D_MIX = D_SSD + D_ATTN

OFF_Z = 0
OFF_Q = OFF_Z + D_SSD
OFF_XBC = OFF_Q + D_ATTN
OFF_DT = OFF_XBC + D_XBC
OFF_K = OFF_DT + 2 * SSD_HEADS
OFF_V = OFF_K + ATTN_KV_HEADS * HEAD_DIM
D_IN_PROJ = OFF_V + ATTN_KV_HEADS * HEAD_DIM

MOE_GROUPS = 4
EXPERTS_PER_GROUP = 4
N_EXPERTS = MOE_GROUPS * EXPERTS_PER_GROUP
TOP_K_IN_GROUP = 2
D_EXPERT = 256

kernel_name = 'hybrid_ssd_gqa_hmoe_block'

F32 = jnp.float32


def layer_norm(x, g, b, eps=1e-5):
    xf = x.astype(F32)
    mu = jnp.mean(xf, -1, keepdims=True)
    var = jnp.mean(jnp.square(xf - mu), -1, keepdims=True)
    return ((xf - mu) * lax.rsqrt(var + eps) * g + b).astype(x.dtype)


def rms_norm(x, g, eps=1e-6):
    xf = x.astype(F32)
    return (xf * lax.rsqrt(jnp.mean(xf * xf, -1, keepdims=True) + eps) * g).astype(x.dtype)


def dwconv_silu(u, w, b):
    y = lax.conv_general_dilated(u, w[:, None, :], (1,), [(CONV_W // 2, CONV_W // 2)],
                                 dimension_numbers=('NWC', 'WIO', 'NWC'),
                                 feature_group_count=u.shape[-1])
    return jax.nn.silu(y + b)


def axial_rope_tables(n_rows):
    rows = jnp.broadcast_to(jnp.arange(n_rows, dtype=F32)[:, None], (n_rows, GRID_W)).reshape(-1)
    cols = jnp.broadcast_to(jnp.arange(GRID_W, dtype=F32)[None, :], (n_rows, GRID_W)).reshape(-1)
    axis_dim = HEAD_DIM // 2
    inv_freq = jnp.power(ROPE_THETA, -jnp.arange(0, axis_dim, 2, dtype=F32) / axis_dim)
    ang = jnp.concatenate([rows[:, None] * inv_freq, cols[:, None] * inv_freq], -1)
    return jnp.cos(ang), jnp.sin(ang)


def apply_rope(x, cos, sin):
    xf = x.astype(F32).reshape(*x.shape[:-1], HEAD_DIM // 2, 2)
    x0, x1 = xf[..., 0], xf[..., 1]
    c = cos[None, :, None, :]
    s = sin[None, :, None, :]
    out = jnp.stack([x0 * c - x1 * s, x0 * s + x1 * c], -1)
    return out.reshape(x.shape).astype(x.dtype)


def ssd_scan(xs, dt, A, Bm, Cm, h0, with_output):
    b, L = xs.shape[:2]
    nc = L // CHUNK
    G, R, P, N = SSD_GROUPS, SSD_HEADS_PER_GROUP, SSD_HEAD_DIM, D_STATE
    xdt = (xs.astype(F32) * dt[..., None]).reshape(b, nc, CHUNK, G, R, P)
    a_cs = jnp.cumsum((dt * A).reshape(b, nc, CHUNK, G, R), axis=2)
    Bc = Bm.astype(F32).reshape(b, nc, CHUNK, G, N)
    w_state = jnp.exp(a_cs[:, :, -1:] - a_cs)
    chunk_states = jnp.einsum('bcsgn,bcsgr,bcsgrp->bcgrpn', Bc, w_state, xdt)
    chunk_decay = jnp.exp(a_cs[:, :, -1])

    def carry_state(h, inp):
        s_c, d_c = inp
        return h * d_c[..., None, None] + s_c, h

    h_final, h_in = lax.scan(carry_state, h0,
                             (jnp.moveaxis(chunk_states, 1, 0), jnp.moveaxis(chunk_decay, 1, 0)))
    if not with_output:
        return h_final
    Cc = Cm.astype(F32).reshape(b, nc, CHUNK, G, N)
    diff = a_cs[:, :, :, None] - a_cs[:, :, None]
    lower = jnp.tril(jnp.ones((CHUNK, CHUNK), bool))[:, :, None, None]
    decay = jnp.where(lower, jnp.exp(jnp.where(lower, diff, 0.0)), 0.0)
    cb = jnp.einsum('bcqgn,bcsgn->bcqsg', Cc, Bc)
    y_diag = jnp.einsum('bcqsgr,bcsgrp->bcqgrp', cb[..., None] * decay, xdt)
    y_off = jnp.einsum('bcqgn,bcgrpn,bcqgr->bcqgrp', Cc, jnp.moveaxis(h_in, 0, 1), jnp.exp(a_cs))
    return (y_diag + y_off).reshape(b, L, G, R, P), h_final


def ssd_bidirectional(xs, Bm, Cm, dt, A, h0_f, h0_b, with_output):
    fl = lambda u: jnp.flip(u, axis=1)
    out_f = ssd_scan(xs, dt[:, :, 0], A[0], Bm, Cm, h0_f, with_output)
    out_b = ssd_scan(fl(xs), fl(dt[:, :, 1]), A[1], fl(Bm), fl(Cm), h0_b, with_output)
    return out_f, out_b


def ssd_prepare(xbc_raw, dt_raw, p):
    b, L, _ = xbc_raw.shape
    xbc = dwconv_silu(xbc_raw, p['conv_w'], p['conv_b'])
    gn = SSD_GROUPS * D_STATE
    xs = xbc[..., :D_SSD].reshape(b, L, SSD_GROUPS, SSD_HEADS_PER_GROUP, SSD_HEAD_DIM)
    Bm = xbc[..., D_SSD:D_SSD + gn].reshape(b, L, SSD_GROUPS, D_STATE)
    Cm = xbc[..., D_SSD + gn:].reshape(b, L, SSD_GROUPS, D_STATE)
    dt = jax.nn.softplus(dt_raw.astype(F32).reshape(b, L, 2, SSD_HEADS) + p['dt_bias'].astype(F32))
    dt = dt.reshape(b, L, 2, SSD_GROUPS, SSD_HEADS_PER_GROUP)
    A = -jnp.exp(p['a_log'].astype(F32)).reshape(2, SSD_GROUPS, SSD_HEADS_PER_GROUP)
    return xs, Bm, Cm, dt, A


def ssd_output(y_f, y_b, xs, z, p):
    b, L = z.shape[:2]
    skip = p['d_skip'].astype(F32).reshape(SSD_GROUPS, SSD_HEADS_PER_GROUP)[..., None]
    y = y_f + jnp.flip(y_b, axis=1) + xs.astype(F32) * skip
    y = y.reshape(b, L, D_SSD) * jax.nn.silu(z.astype(F32))
    yg = y.reshape(b, L, SSD_GROUPS, D_SSD // SSD_GROUPS)
    yg = yg * lax.rsqrt(jnp.mean(yg * yg, -1, keepdims=True) + 1e-6)
    return (yg.reshape(b, L, D_SSD) * p['ssd_norm_g']).astype(z.dtype)


def gqa_block_attention(q, k, v):
    b, L = q.shape[:2]
    nblk = L // Q_BLOCK
    qb = q.reshape(b, nblk, Q_BLOCK, ATTN_KV_HEADS, GQA_GROUP, HEAD_DIM).swapaxes(0, 1)
    scale = HEAD_DIM ** -0.5

    def one_block(q_blk):
        s = jnp.einsum('bqhgd,bkhd->bhgqk', q_blk, k, preferred_element_type=F32) * scale
        pr = jax.nn.softmax(s, axis=-1).astype(v.dtype)
        return jnp.einsum('bhgqk,bkhd->bqhgd', pr, v)

    o = lax.map(one_block, qb)
    return o.swapaxes(0, 1).reshape(b, L, D_ATTN)


def context_mixer(hc, p, with_output):
    b, S, _ = hc.shape
    side = hc @ p['w_in'][:, OFF_XBC:]
    xbc_raw = side[..., :OFF_DT - OFF_XBC]
    dt_raw = side[..., OFF_DT - OFF_XBC:OFF_K - OFF_XBC]
    k = rms_norm(side[..., OFF_K - OFF_XBC:OFF_V - OFF_XBC].reshape(b, S, ATTN_KV_HEADS, HEAD_DIM), p['k_norm_g'])
    v = side[..., OFF_V - OFF_XBC:].reshape(b, S, ATTN_KV_HEADS, HEAD_DIM)
    xs, Bm, Cm, dt, A = ssd_prepare(xbc_raw, dt_raw, p)
    h0 = jnp.zeros((b, SSD_GROUPS, SSD_HEADS_PER_GROUP, SSD_HEAD_DIM, D_STATE), F32)
    if not with_output:
        hf, hb = ssd_bidirectional(xs, Bm, Cm, dt, A, h0, h0, False)
        return (k, v, hf, hb), None
    (y_f, hf), (y_b, hb) = ssd_bidirectional(xs, Bm, Cm, dt, A, h0, h0, True)
    zq = hc @ p['w_in'][:, :OFF_XBC]
    z = zq[..., OFF_Z:OFF_Q]
    q = rms_norm(zq[..., OFF_Q:].reshape(b, S, ATTN_Q_HEADS, HEAD_DIM), p['q_norm_g'])
    y_ssd = ssd_output(y_f, y_b, xs, z, p)
    o = gqa_block_attention(q, k, v)
    return (k, v, hf, hb), jnp.concatenate([y_ssd, o], -1) @ p['w_out']


def latent_mixer(hx, ctx_side, p, cos, sin):
    b, L, _ = hx.shape
    k_ctx, v_ctx, hf_ctx, hb_ctx = ctx_side
    proj = hx @ p['w_in']
    z = proj[..., OFF_Z:OFF_Q]
    q = proj[..., OFF_Q:OFF_XBC].reshape(b, L, ATTN_Q_HEADS, HEAD_DIM)
    xbc_raw = proj[..., OFF_XBC:OFF_DT]
    dt_raw = proj[..., OFF_DT:OFF_K]
    k = proj[..., OFF_K:OFF_V].reshape(b, L, ATTN_KV_HEADS, HEAD_DIM)
    v = proj[..., OFF_V:].reshape(b, L, ATTN_KV_HEADS, HEAD_DIM)
    xs, Bm, Cm, dt, A = ssd_prepare(xbc_raw, dt_raw, p)
    (y_f, _), (y_b, _) = ssd_bidirectional(xs, Bm, Cm, dt, A, hf_ctx, hb_ctx, True)
    y_ssd = ssd_output(y_f, y_b, xs, z, p)
    q = apply_rope(rms_norm(q, p['q_norm_g']), cos, sin)
    k = apply_rope(rms_norm(k, p['k_norm_g']), cos, sin)
    o = gqa_block_attention(q, jnp.concatenate([k_ctx, k], axis=1), jnp.concatenate([v_ctx, v], axis=1))
    return jnp.concatenate([y_ssd, o], -1) @ p['w_out']


def hier_moe(h, p):
    b, L, d = h.shape
    t = h.reshape(-1, d)
    g_prob = jax.nn.softmax((t @ p['w_rg'] + p['b_rg']).astype(F32), axis=-1)
    g_val, g_idx = lax.top_k(g_prob, 1)
    e_logits = (t @ p['w_re'] + p['b_re']).astype(F32).reshape(-1, MOE_GROUPS, EXPERTS_PER_GROUP)
    e_sel = jnp.take_along_axis(e_logits, g_idx[:, :, None], axis=1)[:, 0]
    e_val, e_idx = lax.top_k(jax.nn.softmax(e_sel, axis=-1), TOP_K_IN_GROUP)
    e_val = e_val / jnp.sum(e_val, -1, keepdims=True)
    w_group = jnp.sum(jax.nn.one_hot(e_idx, EXPERTS_PER_GROUP, dtype=F32) * e_val[..., None], axis=1)
    combine = jax.nn.one_hot(g_idx[:, 0], MOE_GROUPS, dtype=F32)[:, :, None] * g_val[:, :, None] * w_group[:, None, :]
    combine = combine.reshape(-1, N_EXPERTS).astype(t.dtype)
    gate = jnp.einsum('td,edf->tef', t, p['w_gate'])
    up = jnp.einsum('td,edf->tef', t, p['w_up'])
    hid = jax.nn.silu(gate) * up * combine[..., None]
    return jnp.einsum('tef,efd->td', hid, p['w_down']).reshape(b, L, d)


def setup_inputs(seed: int = 0) -> dict:
    key = jax.random.key(seed)
    ks = jax.random.split(key, 28)
    nrm = lambda k, shape, scale: jax.random.normal(k, shape, F32) * scale
    beta = (8.0 * DEPTH) ** -0.25
    dt0 = jnp.exp(jax.random.uniform(ks[9], (DEPTH, 2, SSD_HEADS), F32, math.log(1e-3), math.log(1e-1)))
    dt_bias = dt0 + jnp.log(-jnp.expm1(-dt0))
    a_log = jnp.log(jax.random.uniform(ks[10], (DEPTH, 2, SSD_HEADS), F32, 1.0, 16.0))
    return {
        'x': nrm(ks[0], (BATCH, SEQ, D_MODEL), 1.0),
        'c': nrm(ks[1], (BATCH, D_MODEL), 1.0),
        'ctx': nrm(ks[2], (BATCH, CTX_LEN, D_MODEL), 1.0),
        'c_ctx': nrm(ks[3], (D_MODEL,), 1.0),
        'w_mod': nrm(ks[4], (DEPTH, D_MODEL, 6 * D_MODEL), 0.5 * D_MODEL ** -0.5),
        'b_mod': nrm(ks[5], (DEPTH, 6 * D_MODEL), 0.02),
        'w_in': nrm(ks[6], (DEPTH, D_MODEL, D_IN_PROJ), D_MODEL ** -0.5),
        'conv_w': nrm(ks[7], (DEPTH, CONV_W, D_XBC), CONV_W ** -0.5),
        'conv_b': nrm(ks[8], (DEPTH, D_XBC), 0.02),
        'dt_bias': dt_bias,
        'a_log': a_log,
        'd_skip': 1.0 + nrm(ks[11], (DEPTH, SSD_HEADS), 0.05),
        'ssd_norm_g': 1.0 + nrm(ks[12], (DEPTH, D_SSD), 0.05),
        'q_norm_g': 1.0 + nrm(ks[13], (DEPTH, HEAD_DIM), 0.05),
        'k_norm_g': 1.0 + nrm(ks[14], (DEPTH, HEAD_DIM), 0.05),
        'w_out': nrm(ks[15], (DEPTH, D_MIX, D_MODEL), beta * D_MIX ** -0.5),
        'ln1_g': 1.0 + nrm(ks[16], (DEPTH, D_MODEL), 0.05),
        'ln1_b': nrm(ks[17], (DEPTH, D_MODEL), 0.02),
        'w_rg': nrm(ks[18], (DEPTH, D_MODEL, MOE_GROUPS), D_MODEL ** -0.5),
        'b_rg': nrm(ks[19], (DEPTH, MOE_GROUPS), 0.01),
        'w_re': nrm(ks[20], (DEPTH, D_MODEL, N_EXPERTS), D_MODEL ** -0.5),
        'b_re': nrm(ks[21], (DEPTH, N_EXPERTS), 0.01),
        'w_gate': nrm(ks[22], (DEPTH, N_EXPERTS, D_MODEL, D_EXPERT), D_MODEL ** -0.5),
        'w_up': nrm(ks[23], (DEPTH, N_EXPERTS, D_MODEL, D_EXPERT), D_MODEL ** -0.5),
        'w_down': nrm(ks[24], (DEPTH, N_EXPERTS, D_EXPERT, D_MODEL), beta * D_EXPERT ** -0.5),
        'ln2_g': 1.0 + nrm(ks[25], (DEPTH, D_MODEL), 0.05),
        'ln2_b': nrm(ks[26], (DEPTH, D_MODEL), 0.02),
    }


def reference(x, c, ctx, c_ctx, w_mod, b_mod, w_in, conv_w, conv_b, dt_bias, a_log, d_skip, ssd_norm_g,
              q_norm_g, k_norm_g, w_out, ln1_g, ln1_b, w_rg, b_rg, w_re, b_re, w_gate, w_up, w_down,
              ln2_g, ln2_b):
    n_rows = x.shape[1] // GRID_W
    cos, sin = axial_rope_tables(n_rows)
    alpha = (2.0 * DEPTH) ** 0.25
    silu_c = jax.nn.silu(c)
    silu_cc = jax.nn.silu(c_ctx)
    for l in range(DEPTH):
        last = l == DEPTH - 1
        p = {'w_in': w_in[l], 'conv_w': conv_w[l], 'conv_b': conv_b[l], 'dt_bias': dt_bias[l],
             'a_log': a_log[l], 'd_skip': d_skip[l], 'ssd_norm_g': ssd_norm_g[l], 'q_norm_g': q_norm_g[l],
             'k_norm_g': k_norm_g[l], 'w_out': w_out[l], 'w_rg': w_rg[l], 'b_rg': b_rg[l], 'w_re': w_re[l],
             'b_re': b_re[l], 'w_gate': w_gate[l], 'w_up': w_up[l], 'w_down': w_down[l]}
        sh_a, sc_a, g_a, sh_f, sc_f, g_f = [m[:, None, :] for m in jnp.split(silu_c @ w_mod[l] + b_mod[l], 6, axis=-1)]
        n_mod = 3 if last else 6
        mod_c = jnp.split(silu_cc @ w_mod[l][:, :n_mod * D_MODEL] + b_mod[l][:n_mod * D_MODEL], n_mod, axis=-1)
        hc = ctx * (1.0 + mod_c[1]) + mod_c[0]
        ctx_side, ctx_out = context_mixer(hc, p, not last)
        hx = x * (1.0 + sc_a) + sh_a
        x = layer_norm(alpha * x + g_a * latent_mixer(hx, ctx_side, p, cos, sin), ln1_g[l], ln1_b[l])
        x = layer_norm(alpha * x + g_f * hier_moe(x * (1.0 + sc_f) + sh_f, p), ln2_g[l], ln2_b[l])
        if not last:
            ctx = layer_norm(alpha * ctx + mod_c[2] * ctx_out, ln1_g[l], ln1_b[l])
            ctx = layer_norm(alpha * ctx + mod_c[5] * hier_moe(ctx * (1.0 + mod_c[4]) + mod_c[3], p),
                             ln2_g[l], ln2_b[l])
    return x
```

```python
import functools
import math

import jax
import jax.numpy as jnp
from jax import lax
from jax.experimental import pallas as pl
from jax.experimental.pallas import tpu as pltpu

F32 = jnp.float32
BF16 = jnp.bfloat16

D_MODEL = 1024
GRID_W = 64
SSD_HEADS = 16
SSD_HEAD_DIM = 64
D_SSD = SSD_HEADS * SSD_HEAD_DIM
SSD_GROUPS = 2
SSD_HEADS_PER_GROUP = SSD_HEADS // SSD_GROUPS
D_STATE = 128
CONV_W = 5
CHUNK = 128
D_BC = SSD_GROUPS * D_STATE
D_XBC = D_SSD + 2 * D_BC
ATTN_Q_HEADS = 16
ATTN_KV_HEADS = 4
HEAD_DIM = 64
GQA_GROUP = ATTN_Q_HEADS // ATTN_KV_HEADS
D_ATTN = ATTN_Q_HEADS * HEAD_DIM
D_KV = ATTN_KV_HEADS * HEAD_DIM
ROPE_THETA = 10000.0
MOE_GROUPS = 4
EXPERTS_PER_GROUP = 4
N_EXPERTS = MOE_GROUPS * EXPERTS_PER_GROUP
D_EXPERT = 256

OFF_Q = D_SSD
OFF_XBC = OFF_Q + D_ATTN
OFF_DT = OFF_XBC + D_XBC
OFF_K = OFF_DT + 2 * SSD_HEADS
OFF_V = OFF_K + D_KV

LANES = 128
SUBLANES = 8
MXU_DIM = 256
VMEM_LIMIT = 56 * 1024 * 1024

DT_PAD = LANES
HALO = SUBLANES
NEG_BIG = -1e30


def _dot(a, b):
    return jnp.dot(a, b, preferred_element_type=F32)


def _dot_nt(a, b):
    return lax.dot_general(a, b, (((1,), (1,)), ((), ())), preferred_element_type=F32)


def _split2(a):
    hi = a.astype(BF16)
    lo = (a - hi.astype(F32)).astype(BF16)
    return hi, lo


def _split3(a):
    hi = a.astype(BF16)
    r = a - hi.astype(F32)
    mid = r.astype(BF16)
    lo = (r - mid.astype(F32)).astype(BF16)
    return hi, mid, lo


def _dot_f32(a, b):
    a_hi, a_lo = _split2(a)
    b_hi, b_lo = _split2(b)
    return _dot(a_hi, b_hi) + (_dot(a_hi, b_lo) + _dot(a_lo, b_hi))


def _silu(x):
    return x / (1.0 + jnp.exp(-x))


def _softplus(x):
    return jnp.maximum(x, 0.0) + jnp.log(1.0 + jnp.exp(-jnp.abs(x)))


def _layer_norm(x, g, b):
    mu = jnp.mean(x, -1, keepdims=True)
    xc = x - mu
    var = jnp.mean(xc * xc, -1, keepdims=True)
    return xc * lax.rsqrt(var + 1e-5) * g + b


def _params(*sem):
    return pltpu.CompilerParams(dimension_semantics=sem, vmem_limit_bytes=VMEM_LIMIT)


def _mod_kernel(c_ref, w_ref, b_ref, o_ref):
    o_ref[...] = _dot_f32(_silu(c_ref[...]), w_ref[...]) + b_ref[...]


def _modulation(c_rows, w_mod, b_mod):
    m, d = c_rows.shape
    n = w_mod.shape[1]
    tn = 512
    return pl.pallas_call(
        _mod_kernel,
        grid=(n // tn,),
        in_specs=[pl.BlockSpec((m, d), lambda j: (0, 0)),
                  pl.BlockSpec((d, tn), lambda j: (0, j)),
                  pl.BlockSpec((1, tn), lambda j: (0, j))],
        out_specs=pl.BlockSpec((m, tn), lambda j: (0, j)),
        out_shape=jax.ShapeDtypeStruct((m, n), F32),
        compiler_params=_params("parallel"),
        name="modulation",
    )(c_rows, w_mod, b_mod.reshape(1, n))


def _head_sumsq(x2, ones_bd):
    hi, lo = _split2(x2)
    blocks = []
    for j in range(x2.shape[1] // MXU_DIM):
        sl = slice(j * MXU_DIM, (j + 1) * MXU_DIM)
        blocks.append(_dot(hi[:, sl], ones_bd) + _dot(lo[:, sl], ones_bd))
    return blocks[0] if len(blocks) == 1 else jnp.concatenate(blocks, axis=1)


def _rope_partner(x):
    n = x.shape[1]
    half = HEAD_DIM // 2
    lane = lax.broadcasted_iota(jnp.int32, x.shape, 1)
    first_half = (lane & (HEAD_DIM - 1)) < half
    return jnp.where(first_half, pltpu.roll(x, n - half, 1), pltpu.roll(x, half, 1))


def _norm_rope(x, gain, ones_bd, cos, sin):
    ss = _head_sumsq(x * x, ones_bd)
    xn = x * lax.rsqrt(ss * (1.0 / HEAD_DIM) + 1e-6) * gain
    if cos is not None:
        reps = x.shape[1] // cos.shape[1]
        c = jnp.concatenate([cos] * reps, axis=1) if reps > 1 else cos
        s = jnp.concatenate([sin] * reps, axis=1) if reps > 1 else sin
        xn = xn * c + _rope_partner(xn) * s
    return xn


def _in_proj_kernel(*refs, with_zq, with_rope):
    it = iter(refs)
    x_ref, sc_ref, sh_ref, w_ref, ones_ref, gq_ref, gk_ref = (next(it) for _ in range(7))
    cos_ref = sin_ref = None
    if with_rope:
        cos_ref, sin_ref = next(it), next(it)
    if with_zq:
        z_ref, q_ref = next(it), next(it)
    xbc_ref, k_ref, v_ref, dt_ref = (next(it) for _ in range(4))

    hx = (x_ref[0] * (1.0 + sc_ref[0]) + sh_ref[0]).astype(BF16)
    ones_bd = ones_ref[...]
    cos = cos_ref[...] if with_rope else None
    sin = sin_ref[...] if with_rope else None
    off = 0
    if with_zq:
        z_ref[0] = _dot(hx, w_ref[:, 0:D_SSD]).astype(BF16)
        q = _dot(hx, w_ref[:, D_SSD:D_SSD + D_ATTN])
        q_ref[0] = _norm_rope(q, gq_ref[...], ones_bd, cos, sin).astype(BF16)
        off = D_SSD + D_ATTN
    xbc_ref[0] = _dot(hx, w_ref[:, off:off + D_XBC]).astype(BF16)
    off += D_XBC
    k = _dot(hx, w_ref[:, off:off + D_KV])
    k_ref[0] = _norm_rope(k, gk_ref[...], ones_bd, cos, sin).astype(BF16)
    off += D_KV
    v_ref[0] = _dot(hx, w_ref[:, off:off + D_KV]).astype(BF16)
    off += D_KV
    dt_ref[0] = _dot(hx, w_ref[:, off:off + DT_PAD])


def _in_proj(x, scale, shift, w, ones_bd, gq, gk, cos, sin, *, with_zq, tm):
    b, l, d = x.shape
    nw = w.shape[1]
    with_rope = cos is not None
    per_batch = scale.shape[0] == b and b > 1
    mod_map = (lambda t, i: (i, 0, 0)) if per_batch else (lambda t, i: (0, 0, 0))
    row_map = lambda t, i: (i, t, 0)
    const2 = lambda t, i: (0, 0)
    in_specs = [pl.BlockSpec((1, tm, d), row_map),
                pl.BlockSpec((1, 1, d), mod_map),
                pl.BlockSpec((1, 1, d), mod_map),
                pl.BlockSpec((d, nw), const2),
                pl.BlockSpec((MXU_DIM, MXU_DIM), const2),
                pl.BlockSpec((1, D_ATTN), const2),
                pl.BlockSpec((1, D_KV), const2)]
    args = [x, scale, shift, w, ones_bd, gq, gk]
    if with_rope:
        in_specs += [pl.BlockSpec((tm, cos.shape[1]), lambda t, i: (t, 0))] * 2
        args += [cos, sin]
    out_shape, out_specs = [], []

    def add_out(width, dtype):
        out_shape.append(jax.ShapeDtypeStruct((b, l, width), dtype))
        out_specs.append(pl.BlockSpec((1, tm, width), row_map))

    if with_zq:
        add_out(D_SSD, BF16)
        add_out(D_ATTN, BF16)
    add_out(D_XBC, BF16)
    add_out(D_KV, BF16)
    add_out(D_KV, BF16)
    add_out(DT_PAD, F32)
    return pl.pallas_call(
        functools.partial(_in_proj_kernel, with_zq=with_zq, with_rope=with_rope),
        grid=(l // tm, b),
        in_specs=in_specs,
        out_specs=out_specs,
        out_shape=out_shape,
        compiler_params=_params("parallel", "parallel"),
        name="in_proj_latent" if with_zq else "in_proj_ctx",
    )(*args)


def _conv_silu(pad_ref, row0, convw, convb):
    acc = convb
    for j in range(CONV_W):
        start = row0 + HALO - CONV_W // 2 + j
        acc = acc + convw[j:j + 1, :] * pad_ref[start:start + CHUNK, :]
    return _silu(acc)


def _chunk_tables(dt_raw, dtb, a_neg):
    dtv = _softplus(dt_raw + dtb)
    dta = dtv * a_neg
    qi = lax.broadcasted_iota(jnp.int32, (CHUNK, CHUNK), 0)
    si = lax.broadcasted_iota(jnp.int32, (CHUNK, CHUNK), 1)
    hi, mid, lo = _split3(dta)
    out = []
    for mask in (si <= qi, si >= qi):
        tri = jnp.where(mask, 1.0, 0.0).astype(BF16)
        a = _dot(tri, hi) + (_dot(tri, mid) + _dot(tri, lo))
        out.append((a, a.T, mask))
    return dtv, out


def _ssd_chunk_dir(act, dtv, table, direction, h_ref, want_y):
    a, a_t, mask = table
    last = CHUNK - 1 if direction == 0 else 0
    ys = []
    for g in range(SSD_GROUPS):
        bg = act[:, D_SSD + g * D_STATE:D_SSD + (g + 1) * D_STATE]
        bg_t = bg.T.astype(BF16)
        if want_y:
            cg = act[:, D_SSD + D_BC + g * D_STATE:D_SSD + D_BC + (g + 1) * D_STATE].astype(BF16)
            cb = _dot_nt(cg, bg.astype(BF16))
        for r in range(SSD_HEADS_PER_GROUP):
            h = g * SSD_HEADS_PER_GROUP + r
            hc = h + direction * SSD_HEADS
            col = a[:, hc:hc + 1]
            tot = a_t[hc:hc + 1, last:last + 1]
            xdt = act[:, h * SSD_HEAD_DIM:(h + 1) * SSD_HEAD_DIM] * dtv[:, hc:hc + 1]
            state = h_ref[h]
            if want_y:
                row = a_t[hc:hc + 1, :]
                decay = jnp.exp(jnp.where(mask, col - row, NEG_BIG))
                y_diag = _dot((cb * decay).astype(BF16), xdt.astype(BF16))
                y_off = _dot(cg, state.astype(BF16)) * jnp.exp(col)
                ys.append(y_diag + y_off)
            xw = (xdt * jnp.exp(tot - col)).astype(BF16)
            h_ref[h] = jnp.exp(tot) * state + _dot(bg_t, xw)
    return jnp.concatenate(ys, axis=1) if want_y else None


def _ssd_ctx_kernel(xbc_ref, dt_ref, convw_ref, convb_ref, dtb_ref, a_ref, hout_ref, pad_ref, h_ref, *, n_chunks):
    s = n_chunks * CHUNK
    zeros = jnp.zeros((HALO, D_XBC), F32)
    pad_ref[0:HALO, :] = zeros
    pad_ref[HALO + s:HALO + s + HALO, :] = zeros
    pad_ref[HALO:HALO + s, :] = xbc_ref[0].astype(F32)
    convw, convb, dtb, a_neg = convw_ref[...], convb_ref[...], dtb_ref[...], a_ref[...]
    for direction in range(2):
        h_ref[...] = jnp.zeros(h_ref.shape, F32)
        order = range(n_chunks) if direction == 0 else range(n_chunks - 1, -1, -1)
        for c in order:
            act = _conv_silu(pad_ref, c * CHUNK, convw, convb)
            dtv, tables = _chunk_tables(dt_ref[0, c * CHUNK:(c + 1) * CHUNK, :], dtb, a_neg)
            _ssd_chunk_dir(act, dtv, tables[direction], direction, h_ref, False)
        hout_ref[0, direction] = h_ref[...]


def _ssd_ctx(xbc, dt, convw, convb, dtb, a_neg):
    b, s, _ = xbc.shape
    n_chunks = s // CHUNK
    const2 = lambda i: (0, 0)
    return pl.pallas_call(
        functools.partial(_ssd_ctx_kernel, n_chunks=n_chunks),
        grid=(b,),
        in_specs=[pl.BlockSpec((1, s, D_XBC), lambda i: (i, 0, 0)),
                  pl.BlockSpec((1, s, DT_PAD), lambda i: (i, 0, 0)),
                  pl.BlockSpec((CONV_W, D_XBC), const2),
                  pl.BlockSpec((1, D_XBC), const2),
                  pl.BlockSpec((1, DT_PAD), const2),
                  pl.BlockSpec((1, DT_PAD), const2)],
        out_specs=pl.BlockSpec((1, 2, SSD_HEADS, D_STATE, SSD_HEAD_DIM), lambda i: (i, 0, 0, 0, 0)),
        out_shape=jax.ShapeDtypeStruct((b, 2, SSD_HEADS, D_STATE, SSD_HEAD_DIM), F32),
        scratch_shapes=[pltpu.VMEM((s + 2 * HALO, D_XBC), F32),
                        pltpu.VMEM((SSD_HEADS, D_STATE, SSD_HEAD_DIM), F32)],
        compiler_params=_params("parallel"),
        name="ssd_ctx",
    )(xbc, dt, convw, convb, dtb, a_neg)


def _ssd_kernel(xprev_ref, xmain_ref, xnext_ref, dt_ref, z_ref, h0_ref, convw_ref, convb_ref, dtb_ref, a_ref,
                skip_ref, ng_ref, out_ref, pad_ref, act_ref, y_ref, h_ref, *, n_chunks):
    j = pl.program_id(1)
    dtb, a_neg = dtb_ref[...], a_ref[...]

    @pl.when(j < n_chunks)
    def _forward():
        c = j

        @pl.when(j == 0)
        def _():
            h_ref[...] = h0_ref[0, 0]

        pad_ref[0:HALO, :] = jnp.where(c > 0, xprev_ref[0].astype(F32), 0.0)
        pad_ref[HALO:HALO + CHUNK, :] = xmain_ref[0].astype(F32)
        pad_ref[HALO + CHUNK:HALO + CHUNK + HALO, :] = jnp.where(c < n_chunks - 1, xnext_ref[0].astype(F32), 0.0)
        act = _conv_silu(pad_ref, 0, convw_ref[...], convb_ref[...])
        rows = pl.ds(pl.multiple_of(c * CHUNK, CHUNK), CHUNK)
        act_ref[rows, :] = act
        dtv, tables = _chunk_tables(dt_ref[0], dtb, a_neg)
        y = _ssd_chunk_dir(act, dtv, tables[0], 0, h_ref, True)
        y_ref[rows, :] = y + act[:, 0:D_SSD] * skip_ref[...]

    @pl.when(j >= n_chunks)
    def _backward():
        c = 2 * n_chunks - 1 - j

        @pl.when(j == n_chunks)
        def _():
            h_ref[...] = h0_ref[0, 1]

        rows = pl.ds(pl.multiple_of(c * CHUNK, CHUNK), CHUNK)
        act = act_ref[rows, :]
        dtv, tables = _chunk_tables(dt_ref[0], dtb, a_neg)
        y = y_ref[rows, :] + _ssd_chunk_dir(act, dtv, tables[1], 1, h_ref, True)
        y = y * _silu(z_ref[0].astype(F32))
        width = D_SSD // SSD_GROUPS
        parts = []
        for g in range(SSD_GROUPS):
            yg = y[:, g * width:(g + 1) * width]
            parts.append(yg * lax.rsqrt(jnp.mean(yg * yg, -1, keepdims=True) + 1e-6))
        out_ref[0] = (jnp.concatenate(parts, axis=1) * ng_ref[...]).astype(BF16)


def _ssd(xbc, dt, z, h0, convw, convb, dtb, a_neg, skip, ng):
    b, l, _ = xbc.shape
    nc = l // CHUNK
    per = CHUNK // HALO

    def chunk_of(j):
        return jnp.where(j < nc, j, 2 * nc - 1 - j)

    def fwd_chunk(j):
        return jnp.minimum(j, nc - 1)

    def bwd_chunk(j):
        return jnp.where(j < nc, nc - 1, 2 * nc - 1 - j)

    const2 = lambda i, j: (0, 0)
    return pl.pallas_call(
        functools.partial(_ssd_kernel, n_chunks=nc),
        grid=(b, 2 * nc),
        in_specs=[pl.BlockSpec((1, HALO, D_XBC), lambda i, j: (i, jnp.maximum(fwd_chunk(j) * per - 1, 0), 0)),
                  pl.BlockSpec((1, CHUNK, D_XBC), lambda i, j: (i, fwd_chunk(j), 0)),
                  pl.BlockSpec((1, HALO, D_XBC),
                               lambda i, j: (i, jnp.minimum((fwd_chunk(j) + 1) * per, nc * per - 1), 0)),
                  pl.BlockSpec((1, CHUNK, DT_PAD), lambda i, j: (i, chunk_of(j), 0)),
                  pl.BlockSpec((1, CHUNK, D_SSD), lambda i, j: (i, bwd_chunk(j), 0)),
                  pl.BlockSpec((1, 2, SSD_HEADS, D_STATE, SSD_HEAD_DIM), lambda i, j: (i, 0, 0, 0, 0)),
                  pl.BlockSpec((CONV_W, D_XBC), const2),
                  pl.BlockSpec((1, D_XBC), const2),
                  pl.BlockSpec((1, DT_PAD), const2),
                  pl.BlockSpec((1, DT_PAD), const2),
                  pl.BlockSpec((1, D_SSD), const2),
                  pl.BlockSpec((1, D_SSD), const2)],
        out_specs=pl.BlockSpec((1, CHUNK, D_SSD), lambda i, j: (i, bwd_chunk(j), 0)),
        out_shape=jax.ShapeDtypeStruct((b, l, D_SSD), BF16),
        scratch_shapes=[pltpu.VMEM((CHUNK + 2 * HALO, D_XBC), F32),
                        pltpu.VMEM((l, D_XBC), F32),
                        pltpu.VMEM((l, D_SSD), F32),
                        pltpu.VMEM((SSD_HEADS, D_STATE, SSD_HEAD_DIM), F32)],
        compiler_params=_params("parallel", "arbitrary"),
        name="ssd",
    )(xbc, xbc, xbc, dt, z, h0, convw, convb, dtb, a_neg, skip, ng)


def _attn_kernel(q_ref, k_ref, v_ref, o_ref):
    for kvh in range(ATTN_KV_HEADS):
        k = k_ref[0, kvh]
        v = v_ref[0, kvh]
        for g in range(GQA_GROUP):
            h = kvh * GQA_GROUP + g
            q = q_ref[0, :, h * HEAD_DIM:(h + 1) * HEAD_DIM]
            s = _dot_nt(q, k)
            p = jnp.exp(s - jnp.max(s, -1, keepdims=True))
            denom = jnp.sum(p, -1, keepdims=True)
            o = _dot(p.astype(BF16), v) / denom
            o_ref[0, :, h * HEAD_DIM:(h + 1) * HEAD_DIM] = o.astype(BF16)


def _attention(q, k, v, *, tq):
    b, l, _ = q.shape
    s = k.shape[2]
    return pl.pallas_call(
        _attn_kernel,
        grid=(b, l // tq),
        in_specs=[pl.BlockSpec((1, tq, D_ATTN), lambda i, t: (i, t, 0)),
                  pl.BlockSpec((1, ATTN_KV_HEADS, s, HEAD_DIM), lambda i, t: (i, 0, 0, 0)),
                  pl.BlockSpec((1, ATTN_KV_HEADS, s, HEAD_DIM), lambda i, t: (i, 0, 0, 0))],
        out_specs=pl.BlockSpec((1, tq, D_ATTN), lambda i, t: (i, t, 0)),
        out_shape=jax.ShapeDtypeStruct((b, l, D_ATTN), BF16),
        compiler_params=_params("parallel", "parallel"),
        name="attention",
    )(q, k, v)


def _first_index_of_max(vals, lane):
    m = jnp.max(vals, -1, keepdims=True)
    idx = jnp.min(jnp.where(vals == m, lane, LANES), -1, keepdims=True)
    return m, idx


def _router(logits):
    lane = lax.broadcasted_iota(jnp.int32, logits.shape, 1)
    is_e = lane < N_EXPERTS
    is_g = (lane >= N_EXPERTS) & (lane < 2 * N_EXPERTS)
    gl = jnp.where(is_g, logits, NEG_BIG)
    g_max, g_lane = _first_index_of_max(gl, lane)
    g_sum = jnp.sum(jnp.where(is_g, jnp.exp(gl - g_max), 0.0), -1, keepdims=True) * (1.0 / EXPERTS_PER_GROUP)
    g_val = 1.0 / g_sum
    g_idx = (g_lane - N_EXPERTS) // EXPERTS_PER_GROUP
    in_group = is_e & ((lane // EXPERTS_PER_GROUP) == g_idx)
    el = jnp.where(in_group, logits, NEG_BIG)
    m1, i1 = _first_index_of_max(el, lane)
    el2 = jnp.where(lane == i1, NEG_BIG, el)
    m2, i2 = _first_index_of_max(el2, lane)
    r = jnp.exp(m2 - m1)
    w1 = 1.0 / (1.0 + r)
    w2 = r * w1
    return g_val * (jnp.where(lane == i1, w1, 0.0) + jnp.where(lane == i2, w2, 0.0))


def _out_proj_kernel(y_ref, o_ref, x_ref, ga_ref, scf_ref, shf_ref, w_ref, g1_ref, b1_ref, wr_ref, br_ref,
                     x1_ref, t_ref, comb_ref, *, alpha):
    mix = _dot(y_ref[0], w_ref[0:D_SSD, :]) + _dot(o_ref[0], w_ref[D_SSD:D_SSD + D_ATTN, :])
    x1 = _layer_norm(alpha * x_ref[0] + ga_ref[0] * mix, g1_ref[...], b1_ref[...])
    x1_ref[0] = x1
    t = x1 * (1.0 + scf_ref[0]) + shf_ref[0]
    t_ref[0] = t.astype(BF16)
    comb_ref[0] = _router(_dot_f32(t, wr_ref[...]) + br_ref[...])


def _out_proj(y, o, x, ga, scf, shf, w_out, g1, b1, wr, br, *, alpha, tm):
    b, l, d = x.shape
    row_map = lambda i, t: (i, t, 0)
    mod_map = lambda i, t: (i, 0, 0)
    const2 = lambda i, t: (0, 0)
    return pl.pallas_call(
        functools.partial(_out_proj_kernel, alpha=alpha),
        grid=(b, l // tm),
        in_specs=[pl.BlockSpec((1, tm, D_SSD), row_map),
                  pl.BlockSpec((1, tm, D_ATTN), row_map),
                  pl.BlockSpec((1, tm, d), row_map),
                  pl.BlockSpec((1, 1, d), mod_map),
                  pl.BlockSpec((1, 1, d), mod_map),
                  pl.BlockSpec((1, 1, d), mod_map),
                  pl.BlockSpec((D_SSD + D_ATTN, d), const2),
                  pl.BlockSpec((1, d), const2),
                  pl.BlockSpec((1, d), const2),
                  pl.BlockSpec((d, LANES), const2),
                  pl.BlockSpec((1, LANES), const2)],
        out_specs=[pl.BlockSpec((1, tm, d), row_map),
                   pl.BlockSpec((1, tm, d), row_map),
                   pl.BlockSpec((1, tm, LANES), row_map)],
        out_shape=[jax.ShapeDtypeStruct((b, l, d), F32),
                   jax.ShapeDtypeStruct((b, l, d), BF16),
                   jax.ShapeDtypeStruct((b, l, LANES), F32)],
        compiler_params=_params("parallel", "parallel"),
        name="out_proj",
    )(y, o, x, ga, scf, shf, w_out, g1, b1, wr, br)


def _moe_kernel(t_ref, comb_ref, x1_ref, gf_ref, wgu_ref, wd_ref, g2_ref, b2_ref, out_ref, acc_ref, *, alpha):
    e = pl.program_id(2)

    @pl.when(e == 0)
    def _():
        acc_ref[...] = jnp.zeros(acc_ref.shape, F32)

    gu = _dot(t_ref[0], wgu_ref[0])
    comb = comb_ref[0]
    lane = lax.broadcasted_iota(jnp.int32, comb.shape, 1)
    ce = jnp.sum(jnp.where(lane == e, comb, 0.0), -1, keepdims=True)
    hid = _silu(gu[:, 0:D_EXPERT]) * gu[:, D_EXPERT:2 * D_EXPERT] * ce
    acc_ref[...] += _dot(hid.astype(BF16), wd_ref[0])

    @pl.when(e == N_EXPERTS - 1)
    def _():
        out_ref[0] = _layer_norm(alpha * x1_ref[0] + gf_ref[0] * acc_ref[...], g2_ref[...], b2_ref[...])


def _moe(t, comb, x1, gf, wgu, wd, g2, b2, *, alpha, tm):
    b, l, d = x1.shape
    row_map = lambda i, r, e: (i, r, 0)
    const2 = lambda i, r, e: (0, 0)
    return pl.pallas_call(
        functools.partial(_moe_kernel, alpha=alpha),
        grid=(b, l // tm, N_EXPERTS),
        in_specs=[pl.BlockSpec((1, tm, d), row_map),
                  pl.BlockSpec((1, tm, LANES), row_map),
                  pl.BlockSpec((1, tm, d), row_map),
                  pl.BlockSpec((1, 1, d), lambda i, r, e: (i, 0, 0)),
                  pl.BlockSpec((1, d, 2 * D_EXPERT), lambda i, r, e: (e, 0, 0)),
                  pl.BlockSpec((1, D_EXPERT, d), lambda i, r, e: (e, 0, 0)),
                  pl.BlockSpec((1, d), const2),
                  pl.BlockSpec((1, d), const2)],
        out_specs=pl.BlockSpec((1, tm, d), row_map),
        out_shape=jax.ShapeDtypeStruct((b, l, d), F32),
        scratch_shapes=[pltpu.VMEM((tm, d), F32)],
        compiler_params=_params("parallel", "parallel", "arbitrary"),
        name="moe",
    )(t, comb, x1, gf, wgu, wd, g2, b2)


def _rope_tables(n_rows):
    rows = jnp.broadcast_to(jnp.arange(n_rows, dtype=F32)[:, None], (n_rows, GRID_W)).reshape(-1)
    cols = jnp.broadcast_to(jnp.arange(GRID_W, dtype=F32)[None, :], (n_rows, GRID_W)).reshape(-1)
    axis_dim = HEAD_DIM // 2
    inv_freq = jnp.power(ROPE_THETA, -jnp.arange(0, axis_dim, 2, dtype=F32) / axis_dim)
    ang = jnp.concatenate([rows[:, None] * inv_freq, cols[:, None] * inv_freq], -1)
    cos, sin = jnp.cos(ang), jnp.sin(ang)
    cos_h = jnp.concatenate([cos, cos], -1)
    sin_h = jnp.concatenate([-sin, sin], -1)
    reps = LANES // HEAD_DIM
    return jnp.tile(cos_h, (1, reps)), jnp.tile(sin_h, (1, reps))


def _head_perm(n_heads):
    within = jnp.concatenate([jnp.arange(0, HEAD_DIM, 2), jnp.arange(1, HEAD_DIM, 2)])
    return (jnp.arange(n_heads)[:, None] * HEAD_DIM + within[None, :]).reshape(-1)


def kernel(x, c, ctx, c_ctx, w_mod, b_mod, w_in, conv_w, conv_b, dt_bias, a_log, d_skip, ssd_norm_g, q_norm_g,
           k_norm_g, w_out, ln1_g, ln1_b, w_rg, b_rg, w_re, b_re, w_gate, w_up, w_down, ln2_g, ln2_b):
    depth = w_in.shape[0]
    assert depth == 1, "single-layer block only"
    b, l, d = x.shape
    s_ctx = ctx.shape[1]
    assert d == D_MODEL and l % CHUNK == 0 and s_ctx % CHUNK == 0 and l % GRID_W == 0
    alpha = (2.0 * depth) ** 0.25

    m_rows = -(-(b + 1) // SUBLANES) * SUBLANES
    c_rows = jnp.zeros((m_rows, d), F32).at[:b].set(c).at[b].set(c_ctx)
    mod = _modulation(c_rows, w_mod[0], b_mod[0])
    sh_a, sc_a, g_a, sh_f, sc_f, g_f = [mod[:b, i * d:(i + 1) * d].reshape(b, 1, d) for i in range(6)]
    sh_c = mod[b:b + 1, 0:d].reshape(1, 1, d)
    sc_c = mod[b:b + 1, d:2 * d].reshape(1, 1, d)

    w = w_in[0]
    perm_q, perm_k = _head_perm(ATTN_Q_HEADS), _head_perm(ATTN_KV_HEADS)
    w_dt = jnp.pad(w[:, OFF_DT:OFF_K], ((0, 0), (0, DT_PAD - 2 * SSD_HEADS)))
    w_side = jnp.concatenate([w[:, OFF_XBC:OFF_DT], w[:, OFF_K:OFF_V][:, perm_k], w[:, OFF_V:], w_dt], axis=1)
    w_full = jnp.concatenate([w[:, :OFF_Q], w[:, OFF_Q:OFF_XBC][:, perm_q], w_side], axis=1).astype(BF16)
    w_side = w_side.astype(BF16)
    within = perm_k[:HEAD_DIM]
    gq = jnp.tile(q_norm_g[0][within], ATTN_Q_HEADS).reshape(1, D_ATTN) * (HEAD_DIM ** -0.5)
    gk = jnp.tile(k_norm_g[0][within], ATTN_KV_HEADS).reshape(1, D_KV)
    lane = jnp.arange(MXU_DIM)
    ones_bd = (lane[:, None] // HEAD_DIM == lane[None, :] // HEAD_DIM).astype(BF16)
    cos, sin = _rope_tables(l // GRID_W)

    tm_in = 512 if l % 512 == 0 else CHUNK
    tm_ctx = 256 if s_ctx % 256 == 0 else CHUNK
    xbc_c, k_c, v_c, dt_c = _in_proj(ctx, sc_c, sh_c, w_side, ones_bd, gq, gk, None, None, with_zq=False, tm=tm_ctx)
    z, q, xbc, k, v, dt = _in_proj(x, sc_a, sh_a, w_full, ones_bd, gq, gk, cos, sin, with_zq=True, tm=tm_in)

    pad_dt = lambda u: jnp.pad(u.reshape(1, 2 * SSD_HEADS).astype(F32), ((0, 0), (0, DT_PAD - 2 * SSD_HEADS)))
    dtb = pad_dt(dt_bias[0])
    a_neg = pad_dt(-jnp.exp(a_log[0].astype(F32)))
    convw, convb = conv_w[0], conv_b[0].reshape(1, D_XBC)
    skip = jnp.repeat(d_skip[0].astype(F32), SSD_HEAD_DIM).reshape(1, D_SSD)
    ng = ssd_norm_g[0].reshape(1, D_SSD)
    h_ctx = _ssd_ctx(xbc_c, dt_c, convw, convb, dtb, a_neg)
    y_ssd = _ssd(xbc, dt, z, h_ctx, convw, convb, dtb, a_neg, skip, ng)

    to_heads = lambda u: u.reshape(b, -1, ATTN_KV_HEADS, HEAD_DIM).transpose(0, 2, 1, 3)
    k_all = to_heads(jnp.concatenate([k_c, k], axis=1))
    v_all = to_heads(jnp.concatenate([v_c, v], axis=1))
    o = _attention(q, k_all, v_all, tq=CHUNK)

    wr = jnp.concatenate([w_re[0], jnp.repeat(w_rg[0], EXPERTS_PER_GROUP, axis=1)], axis=1)
    wr = jnp.pad(wr, ((0, 0), (0, LANES - 2 * N_EXPERTS)))
    br = jnp.concatenate([b_re[0], jnp.repeat(b_rg[0], EXPERTS_PER_GROUP)])
    br = jnp.pad(br, (0, LANES - 2 * N_EXPERTS)).reshape(1, LANES)
    x1, t, comb = _out_proj(y_ssd, o, x, g_a, sc_f, sh_f, w_out[0].astype(BF16), ln1_g[0].reshape(1, d),
                            ln1_b[0].reshape(1, d), wr, br, alpha=alpha, tm=tm_in)

    wgu = jnp.concatenate([w_gate[0], w_up[0]], axis=-1).astype(BF16)
    tm_moe = 1024 if l % 1024 == 0 else CHUNK
    return _moe(t, comb, x1, g_f, wgu, w_down[0].astype(BF16), ln2_g[0].reshape(1, d), ln2_b[0].reshape(1, d),
                alpha=alpha, tm=tm_moe)
```

```python
import functools
import math

import jax
import jax.numpy as jnp
from jax import lax
from jax.experimental import pallas as pl
from jax.experimental.pallas import tpu as pltpu

F32 = jnp.float32
BF16 = jnp.bfloat16

D_MODEL = 1024
GRID_W = 64
SSD_HEADS = 16
SSD_HEAD_DIM = 64
D_SSD = SSD_HEADS * SSD_HEAD_DIM
SSD_GROUPS = 2
SSD_HEADS_PER_GROUP = SSD_HEADS // SSD_GROUPS
D_STATE = 128
CONV_W = 5
CHUNK = 128
D_BC = SSD_GROUPS * D_STATE
D_XBC = D_SSD + 2 * D_BC
ATTN_Q_HEADS = 16
ATTN_KV_HEADS = 4
HEAD_DIM = 64
GQA_GROUP = ATTN_Q_HEADS // ATTN_KV_HEADS
D_ATTN = ATTN_Q_HEADS * HEAD_DIM
D_KV = ATTN_KV_HEADS * HEAD_DIM
ROPE_THETA = 10000.0
MOE_GROUPS = 4
EXPERTS_PER_GROUP = 4
N_EXPERTS = MOE_GROUPS * EXPERTS_PER_GROUP
D_EXPERT = 256

OFF_Q = D_SSD
OFF_XBC = OFF_Q + D_ATTN
OFF_DT = OFF_XBC + D_XBC
OFF_K = OFF_DT + 2 * SSD_HEADS
OFF_V = OFF_K + D_KV

LANES = 128
SUBLANES = 8
MXU_DIM = 256
VMEM_LIMIT = 56 * 1024 * 1024

DT_PAD = LANES
HALO = SUBLANES
NEG_BIG = -1e30


def _dot(a, b):
    return jnp.dot(a, b, preferred_element_type=F32)


def _dot_nt(a, b):
    return lax.dot_general(a, b, (((1,), (1,)), ((), ())), preferred_element_type=F32)


def _split2(a):
    hi = a.astype(BF16)
    lo = (a - hi.astype(F32)).astype(BF16)
    return hi, lo


def _split3(a):
    hi = a.astype(BF16)
    r = a - hi.astype(F32)
    mid = r.astype(BF16)
    lo = (r - mid.astype(F32)).astype(BF16)
    return hi, mid, lo


def _dot_f32(a, b):
    a_hi, a_lo = _split2(a)
    b_hi, b_lo = _split2(b)
    return _dot(a_hi, b_hi) + (_dot(a_hi, b_lo) + _dot(a_lo, b_hi))


def _silu(x):
    return x / (1.0 + jnp.exp(-x))


def _softplus(x):
    return jnp.maximum(x, 0.0) + jnp.log(1.0 + jnp.exp(-jnp.abs(x)))


def _layer_norm(x, g, b):
    mu = jnp.mean(x, -1, keepdims=True)
    xc = x - mu
    var = jnp.mean(xc * xc, -1, keepdims=True)
    return xc * lax.rsqrt(var + 1e-5) * g + b


def _params(*sem):
    return pltpu.CompilerParams(dimension_semantics=sem, vmem_limit_bytes=VMEM_LIMIT)


def _mod_kernel(c_ref, w_ref, b_ref, o_ref):
    o_ref[...] = _dot_f32(_silu(c_ref[...]), w_ref[...]) + b_ref[...]


def _modulation(c_rows, w_mod, b_mod):
    m, d = c_rows.shape
    n = w_mod.shape[1]
    tn = 512
    return pl.pallas_call(
        _mod_kernel,
        grid=(n // tn,),
        in_specs=[pl.BlockSpec((m, d), lambda j: (0, 0)),
                  pl.BlockSpec((d, tn), lambda j: (0, j)),
                  pl.BlockSpec((1, tn), lambda j: (0, j))],
        out_specs=pl.BlockSpec((m, tn), lambda j: (0, j)),
        out_shape=jax.ShapeDtypeStruct((m, n), F32),
        compiler_params=_params("parallel"),
        name="modulation",
    )(c_rows, w_mod, b_mod.reshape(1, n))


def _head_sumsq(x2, ones_bd):
    hi, lo = _split2(x2)
    blocks = []
    for j in range(x2.shape[1] // MXU_DIM):
        sl = slice(j * MXU_DIM, (j + 1) * MXU_DIM)
        blocks.append(_dot(hi[:, sl], ones_bd) + _dot(lo[:, sl], ones_bd))
    return blocks[0] if len(blocks) == 1 else jnp.concatenate(blocks, axis=1)


def _rope_partner(x):
    n = x.shape[1]
    half = HEAD_DIM // 2
    lane = lax.broadcasted_iota(jnp.int32, x.shape, 1)
    first_half = (lane & (HEAD_DIM - 1)) < half
    return jnp.where(first_half, pltpu.roll(x, n - half, 1), pltpu.roll(x, half, 1))


def _norm_rope(x, gain, ones_bd, cos, sin):
    ss = _head_sumsq(x * x, ones_bd)
    xn = x * lax.rsqrt(ss * (1.0 / HEAD_DIM) + 1e-6) * gain
    if cos is not None:
        reps = x.shape[1] // cos.shape[1]
        c = jnp.concatenate([cos] * reps, axis=1) if reps > 1 else cos
        s = jnp.concatenate([sin] * reps, axis=1) if reps > 1 else sin
        xn = xn * c + _rope_partner(xn) * s
    return xn


def _in_proj_kernel(*refs, with_zq, with_rope):
    it = iter(refs)
    x_ref, sc_ref, sh_ref, w_ref, ones_ref, gq_ref, gk_ref = (next(it) for _ in range(7))
    cos_ref = sin_ref = None
    if with_rope:
        cos_ref, sin_ref = next(it), next(it)
    if with_zq:
        z_ref, q_ref = next(it), next(it)
    xbc_ref, k_ref, v_ref, dt_ref = (next(it) for _ in range(4))

    hx = (x_ref[0] * (1.0 + sc_ref[0]) + sh_ref[0]).astype(BF16)
    ones_bd = ones_ref[...]
    cos = cos_ref[...] if with_rope else None
    sin = sin_ref[...] if with_rope else None
    off = 0
    if with_zq:
        z_ref[0] = _dot(hx, w_ref[:, 0:D_SSD]).astype(BF16)
        q = _dot(hx, w_ref[:, D_SSD:D_SSD + D_ATTN])
        q_ref[0] = _norm_rope(q, gq_ref[...], ones_bd, cos, sin).astype(BF16)
        off = D_SSD + D_ATTN
    xbc_ref[0] = _dot(hx, w_ref[:, off:off + D_XBC]).astype(BF16)
    off += D_XBC
    k = _dot(hx, w_ref[:, off:off + D_KV])
    k_ref[0] = _norm_rope(k, gk_ref[...], ones_bd, cos, sin).astype(BF16)
    off += D_KV
    v_ref[0] = _dot(hx, w_ref[:, off:off + D_KV]).astype(BF16)
    off += D_KV
    dt_ref[0] = _dot(hx, w_ref[:, off:off + DT_PAD])


def _in_proj(x, scale, shift, w, ones_bd, gq, gk, cos, sin, *, with_zq, tm):
    b, l, d = x.shape
    nw = w.shape[1]
    with_rope = cos is not None
    per_batch = scale.shape[0] == b and b > 1
    mod_map = (lambda t, i: (i, 0, 0)) if per_batch else (lambda t, i: (0, 0, 0))
    row_map = lambda t, i: (i, t, 0)
    const2 = lambda t, i: (0, 0)
    in_specs = [pl.BlockSpec((1, tm, d), row_map),
                pl.BlockSpec((1, 1, d), mod_map),
                pl.BlockSpec((1, 1, d), mod_map),
                pl.BlockSpec((d, nw), const2),
                pl.BlockSpec((MXU_DIM, MXU_DIM), const2),
                pl.BlockSpec((1, D_ATTN), const2),
                pl.BlockSpec((1, D_KV), const2)]
    args = [x, scale, shift, w, ones_bd, gq, gk]
    if with_rope:
        in_specs += [pl.BlockSpec((tm, cos.shape[1]), lambda t, i: (t, 0))] * 2
        args += [cos, sin]
    out_shape, out_specs = [], []

    def add_out(width, dtype):
        out_shape.append(jax.ShapeDtypeStruct((b, l, width), dtype))
        out_specs.append(pl.BlockSpec((1, tm, width), row_map))

    if with_zq:
        add_out(D_SSD, BF16)
        add_out(D_ATTN, BF16)
    add_out(D_XBC, BF16)
    add_out(D_KV, BF16)
    add_out(D_KV, BF16)
    add_out(DT_PAD, F32)
    return pl.pallas_call(
        functools.partial(_in_proj_kernel, with_zq=with_zq, with_rope=with_rope),
        grid=(l // tm, b),
        in_specs=in_specs,
        out_specs=out_specs,
        out_shape=out_shape,
        compiler_params=_params("parallel", "parallel"),
        name="in_proj_latent" if with_zq else "in_proj_ctx",
    )(*args)


def _conv_silu(pad_ref, row0, convw, convb):
    acc = convb
    for j in range(CONV_W):
        start = row0 + HALO - CONV_W // 2 + j
        acc = acc + convw[j:j + 1, :] * pad_ref[start:start + CHUNK, :]
    return _silu(acc)


def _chunk_tables(dt_raw, dtb, a_neg):
    dtv = _softplus(dt_raw + dtb)
    dta = dtv * a_neg
    qi = lax.broadcasted_iota(jnp.int32, (CHUNK, CHUNK), 0)
    si = lax.broadcasted_iota(jnp.int32, (CHUNK, CHUNK), 1)
    hi, mid, lo = _split3(dta)
    out = []
    for mask in (si <= qi, si >= qi):
        tri = jnp.where(mask, 1.0, 0.0).astype(BF16)
        a = _dot(tri, hi) + (_dot(tri, mid) + _dot(tri, lo))
        out.append((a, a.T, mask))
    return dtv, out


def _ssd_chunk_dir(act, dtv, table, direction, h_ref, want_y):
    a, a_t, mask = table
    last = CHUNK - 1 if direction == 0 else 0
    ys = []
    for g in range(SSD_GROUPS):
        bg = act[:, D_SSD + g * D_STATE:D_SSD + (g + 1) * D_STATE]
        bg_t = bg.T.astype(BF16)
        if want_y:
            cg = act[:, D_SSD + D_BC + g * D_STATE:D_SSD + D_BC + (g + 1) * D_STATE].astype(BF16)
            cb = _dot_nt(cg, bg.astype(BF16))
        for r in range(SSD_HEADS_PER_GROUP):
            h = g * SSD_HEADS_PER_GROUP + r
            hc = h + direction * SSD_HEADS
            col = a[:, hc:hc + 1]
            tot = a_t[hc:hc + 1, last:last + 1]
            xdt = act[:, h * SSD_HEAD_DIM:(h + 1) * SSD_HEAD_DIM] * dtv[:, hc:hc + 1]
            state = h_ref[h]
            if want_y:
                row = a_t[hc:hc + 1, :]
                decay = jnp.exp(jnp.where(mask, col - row, NEG_BIG))
                y_diag = _dot((cb * decay).astype(BF16), xdt.astype(BF16))
                y_off = _dot(cg, state.astype(BF16)) * jnp.exp(col)
                ys.append(y_diag + y_off)
            xw = (xdt * jnp.exp(tot - col)).astype(BF16)
            h_ref[h] = jnp.exp(tot) * state + _dot(bg_t, xw)
    return jnp.concatenate(ys, axis=1) if want_y else None


def _ssd_ctx_kernel(xbc_ref, dt_ref, convw_ref, convb_ref, dtb_ref, a_ref, hout_ref, pad_ref, h_ref, *, n_chunks):
    s = n_chunks * CHUNK
    zeros = jnp.zeros((HALO, D_XBC), F32)
    pad_ref[0:HALO, :] = zeros
    pad_ref[HALO + s:HALO + s + HALO, :] = zeros
    pad_ref[HALO:HALO + s, :] = xbc_ref[0].astype(F32)
    convw, convb, dtb, a_neg = convw_ref[...], convb_ref[...], dtb_ref[...], a_ref[...]
    for direction in range(2):
        h_ref[...] = jnp.zeros(h_ref.shape, F32)
        order = range(n_chunks) if direction == 0 else range(n_chunks - 1, -1, -1)
        for c in order:
            act = _conv_silu(pad_ref, c * CHUNK, convw, convb)
            dtv, tables = _chunk_tables(dt_ref[0, c * CHUNK:(c + 1) * CHUNK, :], dtb, a_neg)
            _ssd_chunk_dir(act, dtv, tables[direction], direction, h_ref, False)
        hout_ref[0, direction] = h_ref[...]


def _ssd_ctx(xbc, dt, convw, convb, dtb, a_neg):
    b, s, _ = xbc.shape
    n_chunks = s // CHUNK
    const2 = lambda i: (0, 0)
    return pl.pallas_call(
        functools.partial(_ssd_ctx_kernel, n_chunks=n_chunks),
        grid=(b,),
        in_specs=[pl.BlockSpec((1, s, D_XBC), lambda i: (i, 0, 0)),
                  pl.BlockSpec((1, s, DT_PAD), lambda i: (i, 0, 0)),
                  pl.BlockSpec((CONV_W, D_XBC), const2),
                  pl.BlockSpec((1, D_XBC), const2),
                  pl.BlockSpec((1, DT_PAD), const2),
                  pl.BlockSpec((1, DT_PAD), const2)],
        out_specs=pl.BlockSpec((1, 2, SSD_HEADS, D_STATE, SSD_HEAD_DIM), lambda i: (i, 0, 0, 0, 0)),
        out_shape=jax.ShapeDtypeStruct((b, 2, SSD_HEADS, D_STATE, SSD_HEAD_DIM), F32),
        scratch_shapes=[pltpu.VMEM((s + 2 * HALO, D_XBC), F32),
                        pltpu.VMEM((SSD_HEADS, D_STATE, SSD_HEAD_DIM), F32)],
        compiler_params=_params("parallel"),
        name="ssd_ctx",
    )(xbc, dt, convw, convb, dtb, a_neg)


def _ssd_kernel(xprev_ref, xmain_ref, xnext_ref, dt_ref, z_ref, h0_ref, convw_ref, convb_ref, dtb_ref, a_ref,
                skip_ref, ng_ref, out_ref, pad_ref, act_ref, y_ref, h_ref, *, n_chunks):
    j = pl.program_id(1)
    dtb, a_neg = dtb_ref[...], a_ref[...]

    @pl.when(j < n_chunks)
    def _forward():
        c = j

        @pl.when(j == 0)
        def _():
            h_ref[...] = h0_ref[0, 0]

        pad_ref[0:HALO, :] = jnp.where(c > 0, xprev_ref[0].astype(F32), 0.0)
        pad_ref[HALO:HALO + CHUNK, :] = xmain_ref[0].astype(F32)
        pad_ref[HALO + CHUNK:HALO + CHUNK + HALO, :] = jnp.where(c < n_chunks - 1, xnext_ref[0].astype(F32), 0.0)
        act = _conv_silu(pad_ref, 0, convw_ref[...], convb_ref[...])
        rows = pl.ds(pl.multiple_of(c * CHUNK, CHUNK), CHUNK)
        act_ref[rows, :] = act
        dtv, tables = _chunk_tables(dt_ref[0], dtb, a_neg)
        y = _ssd_chunk_dir(act, dtv, tables[0], 0, h_ref, True)
        y_ref[rows, :] = y + act[:, 0:D_SSD] * skip_ref[...]

    @pl.when(j >= n_chunks)
    def _backward():
        c = 2 * n_chunks - 1 - j

        @pl.when(j == n_chunks)
        def _():
            h_ref[...] = h0_ref[0, 1]

        rows = pl.ds(pl.multiple_of(c * CHUNK, CHUNK), CHUNK)
        act = act_ref[rows, :]
        dtv, tables = _chunk_tables(dt_ref[0], dtb, a_neg)
        y = y_ref[rows, :] + _ssd_chunk_dir(act, dtv, tables[1], 1, h_ref, True)
        y = y * _silu(z_ref[0].astype(F32))
        width = D_SSD // SSD_GROUPS
        parts = []
        for g in range(SSD_GROUPS):
            yg = y[:, g * width:(g + 1) * width]
            parts.append(yg * lax.rsqrt(jnp.mean(yg * yg, -1, keepdims=True) + 1e-6))
        out_ref[0] = (jnp.concatenate(parts, axis=1) * ng_ref[...]).astype(BF16)


def _ssd(xbc, dt, z, h0, convw, convb, dtb, a_neg, skip, ng):
    b, l, _ = xbc.shape
    nc = l // CHUNK
    per = CHUNK // HALO

    def chunk_of(j):
        return jnp.where(j < nc, j, 2 * nc - 1 - j)

    def fwd_chunk(j):
        return jnp.minimum(j, nc - 1)

    def bwd_chunk(j):
        return jnp.where(j < nc, nc - 1, 2 * nc - 1 - j)

    const2 = lambda i, j: (0, 0)
    return pl.pallas_call(
        functools.partial(_ssd_kernel, n_chunks=nc),
        grid=(b, 2 * nc),
        in_specs=[pl.BlockSpec((1, HALO, D_XBC), lambda i, j: (i, jnp.maximum(fwd_chunk(j) * per - 1, 0), 0)),
                  pl.BlockSpec((1, CHUNK, D_XBC), lambda i, j: (i, fwd_chunk(j), 0)),
                  pl.BlockSpec((1, HALO, D_XBC),
                               lambda i, j: (i, jnp.minimum((fwd_chunk(j) + 1) * per, nc * per - 1), 0)),
                  pl.BlockSpec((1, CHUNK, DT_PAD), lambda i, j: (i, chunk_of(j), 0)),
                  pl.BlockSpec((1, CHUNK, D_SSD), lambda i, j: (i, bwd_chunk(j), 0)),
                  pl.BlockSpec((1, 2, SSD_HEADS, D_STATE, SSD_HEAD_DIM), lambda i, j: (i, 0, 0, 0, 0)),
                  pl.BlockSpec((CONV_W, D_XBC), const2),
                  pl.BlockSpec((1, D_XBC), const2),
                  pl.BlockSpec((1, DT_PAD), const2),
                  pl.BlockSpec((1, DT_PAD), const2),
                  pl.BlockSpec((1, D_SSD), const2),
                  pl.BlockSpec((1, D_SSD), const2)],
        out_specs=pl.BlockSpec((1, CHUNK, D_SSD), lambda i, j: (i, bwd_chunk(j), 0)),
        out_shape=jax.ShapeDtypeStruct((b, l, D_SSD), BF16),
        scratch_shapes=[pltpu.VMEM((CHUNK + 2 * HALO, D_XBC), F32),
                        pltpu.VMEM((l, D_XBC), F32),
                        pltpu.VMEM((l, D_SSD), F32),
                        pltpu.VMEM((SSD_HEADS, D_STATE, SSD_HEAD_DIM), F32)],
        compiler_params=_params("parallel", "arbitrary"),
        name="ssd",
    )(xbc, xbc, xbc, dt, z, h0, convw, convb, dtb, a_neg, skip, ng)


def _attn_kernel(q_ref, k_ref, v_ref, o_ref, *, tq):
    for kvh in range(ATTN_KV_HEADS):
        base = kvh * GQA_GROUP * HEAD_DIM
        q = jnp.concatenate([q_ref[0, :, base + g * HEAD_DIM:base + (g + 1) * HEAD_DIM] for g in range(GQA_GROUP)],
                            axis=0)
        s = _dot_nt(q, k_ref[0, kvh])
        p = jnp.exp2(s - jnp.max(s, -1, keepdims=True))
        pv = _dot(p.astype(BF16), v_ref[0, kvh])
        o = pv[:, 0:HEAD_DIM] / pv[:, HEAD_DIM:HEAD_DIM + 1]
        o_ref[0, :, base:base + GQA_GROUP * HEAD_DIM] = jnp.concatenate(
            [o[g * tq:(g + 1) * tq] for g in range(GQA_GROUP)], axis=1).astype(BF16)


def _attention(q, k, v, *, tq):
    b, l, _ = q.shape
    s = k.shape[2]
    return pl.pallas_call(
        functools.partial(_attn_kernel, tq=tq),
        grid=(b, l // tq),
        in_specs=[pl.BlockSpec((1, tq, D_ATTN), lambda i, t: (i, t, 0)),
                  pl.BlockSpec((1, ATTN_KV_HEADS, s, HEAD_DIM), lambda i, t: (i, 0, 0, 0)),
                  pl.BlockSpec((1, ATTN_KV_HEADS, s, LANES), lambda i, t: (i, 0, 0, 0))],
        out_specs=pl.BlockSpec((1, tq, D_ATTN), lambda i, t: (i, t, 0)),
        out_shape=jax.ShapeDtypeStruct((b, l, D_ATTN), BF16),
        compiler_params=_params("parallel", "parallel"),
        name="attention",
    )(q, k, v)


def _first_index_of_max(vals, lane):
    m = jnp.max(vals, -1, keepdims=True)
    idx = jnp.min(jnp.where(vals == m, lane, LANES), -1, keepdims=True)
    return m, idx


def _router(logits):
    lane = lax.broadcasted_iota(jnp.int32, logits.shape, 1)
    is_e = lane < N_EXPERTS
    is_g = (lane >= N_EXPERTS) & (lane < 2 * N_EXPERTS)
    gl = jnp.where(is_g, logits, NEG_BIG)
    g_max, g_lane = _first_index_of_max(gl, lane)
    g_sum = jnp.sum(jnp.where(is_g, jnp.exp(gl - g_max), 0.0), -1, keepdims=True) * (1.0 / EXPERTS_PER_GROUP)
    g_val = 1.0 / g_sum
    g_idx = (g_lane - N_EXPERTS) // EXPERTS_PER_GROUP
    in_group = is_e & ((lane // EXPERTS_PER_GROUP) == g_idx)
    el = jnp.where(in_group, logits, NEG_BIG)
    m1, i1 = _first_index_of_max(el, lane)
    el2 = jnp.where(lane == i1, NEG_BIG, el)
    m2, i2 = _first_index_of_max(el2, lane)
    r = jnp.exp(m2 - m1)
    w1 = 1.0 / (1.0 + r)
    w2 = r * w1
    return g_val * (jnp.where(lane == i1, w1, 0.0) + jnp.where(lane == i2, w2, 0.0))


def _out_proj_kernel(y_ref, o_ref, x_ref, ga_ref, scf_ref, shf_ref, w_ref, g1_ref, b1_ref, wr_ref, br_ref,
                     x1_ref, t_ref, comb_ref, *, alpha):
    mix = _dot(y_ref[0], w_ref[0:D_SSD, :]) + _dot(o_ref[0], w_ref[D_SSD:D_SSD + D_ATTN, :])
    x1 = _layer_norm(alpha * x_ref[0] + ga_ref[0] * mix, g1_ref[...], b1_ref[...])
    x1_ref[0] = x1
    t = x1 * (1.0 + scf_ref[0]) + shf_ref[0]
    t_ref[0] = t.astype(BF16)
    comb_ref[0] = _router(_dot_f32(t, wr_ref[...]) + br_ref[...])


def _out_proj(y, o, x, ga, scf, shf, w_out, g1, b1, wr, br, *, alpha, tm):
    b, l, d = x.shape
    row_map = lambda i, t: (i, t, 0)
    mod_map = lambda i, t: (i, 0, 0)
    const2 = lambda i, t: (0, 0)
    return pl.pallas_call(
        functools.partial(_out_proj_kernel, alpha=alpha),
        grid=(b, l // tm),
        in_specs=[pl.BlockSpec((1, tm, D_SSD), row_map),
                  pl.BlockSpec((1, tm, D_ATTN), row_map),
                  pl.BlockSpec((1, tm, d), row_map),
                  pl.BlockSpec((1, 1, d), mod_map),
                  pl.BlockSpec((1, 1, d), mod_map),
                  pl.BlockSpec((1, 1, d), mod_map),
                  pl.BlockSpec((D_SSD + D_ATTN, d), const2),
                  pl.BlockSpec((1, d), const2),
                  pl.BlockSpec((1, d), const2),
                  pl.BlockSpec((d, LANES), const2),
                  pl.BlockSpec((1, LANES), const2)],
        out_specs=[pl.BlockSpec((1, tm, d), row_map),
                   pl.BlockSpec((1, tm, d), row_map),
                   pl.BlockSpec((1, tm, LANES), row_map)],
        out_shape=[jax.ShapeDtypeStruct((b, l, d), F32),
                   jax.ShapeDtypeStruct((b, l, d), BF16),
                   jax.ShapeDtypeStruct((b, l, LANES), F32)],
        compiler_params=_params("parallel", "parallel"),
        name="out_proj",
    )(y, o, x, ga, scf, shf, w_out, g1, b1, wr, br)


def _moe_kernel(t_ref, comb_ref, x1_ref, gf_ref, wgu_ref, wd_ref, g2_ref, b2_ref, out_ref, acc_ref, *, alpha):
    t = t_ref[0]
    comb = comb_ref[0]
    lane = lax.broadcasted_iota(jnp.int32, comb.shape, 1)
    width = EXPERTS_PER_GROUP * D_EXPERT

    def group_out(g):
        gu = _dot(t, wgu_ref[g])
        weights = []
        for i in range(EXPERTS_PER_GROUP):
            ce = jnp.sum(jnp.where(lane == g * EXPERTS_PER_GROUP + i, comb, 0.0), -1, keepdims=True)
            weights.append(jnp.broadcast_to(ce, (comb.shape[0], D_EXPERT)))
        hid = _silu(gu[:, 0:width]) * gu[:, width:2 * width] * jnp.concatenate(weights, axis=1)
        return _dot(hid.astype(BF16), wd_ref[g])

    acc_ref[...] = group_out(0)

    def body(g, carry):
        acc_ref[...] += group_out(g)
        return carry

    lax.fori_loop(1, MOE_GROUPS, body, 0)
    out_ref[0] = _layer_norm(alpha * x1_ref[0] + gf_ref[0] * acc_ref[...], g2_ref[...], b2_ref[...])


def _moe(t, comb, x1, gf, wgu, wd, g2, b2, *, alpha, tm):
    b, l, d = x1.shape
    row_map = lambda i, r: (i, r, 0)
    const2 = lambda i, r: (0, 0)
    const3 = lambda i, r: (0, 0, 0)
    resident = pl.Buffered(1)
    return pl.pallas_call(
        functools.partial(_moe_kernel, alpha=alpha),
        grid=(b, l // tm),
        in_specs=[pl.BlockSpec((1, tm, d), row_map),
                  pl.BlockSpec((1, tm, LANES), row_map),
                  pl.BlockSpec((1, tm, d), row_map),
                  pl.BlockSpec((1, 1, d), lambda i, r: (i, 0, 0)),
                  pl.BlockSpec(wgu.shape, const3, pipeline_mode=resident),
                  pl.BlockSpec(wd.shape, const3, pipeline_mode=resident),
                  pl.BlockSpec((1, d), const2),
                  pl.BlockSpec((1, d), const2)],
        out_specs=pl.BlockSpec((1, tm, d), row_map),
        out_shape=jax.ShapeDtypeStruct((b, l, d), F32),
        scratch_shapes=[pltpu.VMEM((tm, d), F32)],
        compiler_params=_params("parallel", "parallel"),
        name="moe",
    )(t, comb, x1, gf, wgu, wd, g2, b2)


def _rope_tables(n_rows):
    rows = jnp.broadcast_to(jnp.arange(n_rows, dtype=F32)[:, None], (n_rows, GRID_W)).reshape(-1)
    cols = jnp.broadcast_to(jnp.arange(GRID_W, dtype=F32)[None, :], (n_rows, GRID_W)).reshape(-1)
    axis_dim = HEAD_DIM // 2
    inv_freq = jnp.power(ROPE_THETA, -jnp.arange(0, axis_dim, 2, dtype=F32) / axis_dim)
    ang = jnp.concatenate([rows[:, None] * inv_freq, cols[:, None] * inv_freq], -1)
    cos, sin = jnp.cos(ang), jnp.sin(ang)
    cos_h = jnp.concatenate([cos, cos], -1)
    sin_h = jnp.concatenate([-sin, sin], -1)
    reps = LANES // HEAD_DIM
    return jnp.tile(cos_h, (1, reps)), jnp.tile(sin_h, (1, reps))


def _head_perm(n_heads):
    within = jnp.concatenate([jnp.arange(0, HEAD_DIM, 2), jnp.arange(1, HEAD_DIM, 2)])
    return (jnp.arange(n_heads)[:, None] * HEAD_DIM + within[None, :]).reshape(-1)


def kernel(x, c, ctx, c_ctx, w_mod, b_mod, w_in, conv_w, conv_b, dt_bias, a_log, d_skip, ssd_norm_g, q_norm_g,
           k_norm_g, w_out, ln1_g, ln1_b, w_rg, b_rg, w_re, b_re, w_gate, w_up, w_down, ln2_g, ln2_b):
    depth = w_in.shape[0]
    assert depth == 1, "single-layer block only"
    b, l, d = x.shape
    s_ctx = ctx.shape[1]
    assert d == D_MODEL and l % CHUNK == 0 and s_ctx % CHUNK == 0 and l % GRID_W == 0
    alpha = (2.0 * depth) ** 0.25

    m_rows = -(-(b + 1) // SUBLANES) * SUBLANES
    c_rows = jnp.zeros((m_rows, d), F32).at[:b].set(c).at[b].set(c_ctx)
    mod = _modulation(c_rows, w_mod[0], b_mod[0])
    sh_a, sc_a, g_a, sh_f, sc_f, g_f = [mod[:b, i * d:(i + 1) * d].reshape(b, 1, d) for i in range(6)]
    sh_c = mod[b:b + 1, 0:d].reshape(1, 1, d)
    sc_c = mod[b:b + 1, d:2 * d].reshape(1, 1, d)

    w = w_in[0]
    perm_q, perm_k = _head_perm(ATTN_Q_HEADS), _head_perm(ATTN_KV_HEADS)
    w_dt = jnp.pad(w[:, OFF_DT:OFF_K], ((0, 0), (0, DT_PAD - 2 * SSD_HEADS)))
    w_side = jnp.concatenate([w[:, OFF_XBC:OFF_DT], w[:, OFF_K:OFF_V][:, perm_k], w[:, OFF_V:], w_dt], axis=1)
    w_full = jnp.concatenate([w[:, :OFF_Q], w[:, OFF_Q:OFF_XBC][:, perm_q], w_side], axis=1).astype(BF16)
    w_side = w_side.astype(BF16)
    within = perm_k[:HEAD_DIM]
    gq = jnp.tile(q_norm_g[0][within], ATTN_Q_HEADS).reshape(1, D_ATTN) * (HEAD_DIM ** -0.5 * math.log2(math.e))
    gk = jnp.tile(k_norm_g[0][within], ATTN_KV_HEADS).reshape(1, D_KV)
    lane = jnp.arange(MXU_DIM)
    ones_bd = (lane[:, None] // HEAD_DIM == lane[None, :] // HEAD_DIM).astype(BF16)
    cos, sin = _rope_tables(l // GRID_W)

    tm_in = 512 if l % 512 == 0 else CHUNK
    tm_ctx = 256 if s_ctx % 256 == 0 else CHUNK
    xbc_c, k_c, v_c, dt_c = _in_proj(ctx, sc_c, sh_c, w_side, ones_bd, gq, gk, None, None, with_zq=False, tm=tm_ctx)
    z, q, xbc, k, v, dt = _in_proj(x, sc_a, sh_a, w_full, ones_bd, gq, gk, cos, sin, with_zq=True, tm=tm_in)

    pad_dt = lambda u: jnp.pad(u.reshape(1, 2 * SSD_HEADS).astype(F32), ((0, 0), (0, DT_PAD - 2 * SSD_HEADS)))
    dtb = pad_dt(dt_bias[0])
    a_neg = pad_dt(-jnp.exp(a_log[0].astype(F32)))
    convw, convb = conv_w[0], conv_b[0].reshape(1, D_XBC)
    skip = jnp.repeat(d_skip[0].astype(F32), SSD_HEAD_DIM).reshape(1, D_SSD)
    ng = ssd_norm_g[0].reshape(1, D_SSD)
    h_ctx = _ssd_ctx(xbc_c, dt_c, convw, convb, dtb, a_neg)
    y_ssd = _ssd(xbc, dt, z, h_ctx, convw, convb, dtb, a_neg, skip, ng)

    to_heads = lambda u: u.reshape(b, -1, ATTN_KV_HEADS, HEAD_DIM).transpose(0, 2, 1, 3)
    k_all = to_heads(jnp.concatenate([k_c, k], axis=1))
    v_all = to_heads(jnp.concatenate([v_c, v], axis=1))
    ones_col = (jnp.arange(LANES - HEAD_DIM) == 0).astype(BF16)
    v_all = jnp.concatenate([v_all, jnp.broadcast_to(ones_col, v_all.shape[:-1] + (LANES - HEAD_DIM,))], axis=-1)
    o = _attention(q, k_all, v_all, tq=CHUNK)

    wr = jnp.concatenate([w_re[0], jnp.repeat(w_rg[0], EXPERTS_PER_GROUP, axis=1)], axis=1)
    wr = jnp.pad(wr, ((0, 0), (0, LANES - 2 * N_EXPERTS)))
    br = jnp.concatenate([b_re[0], jnp.repeat(b_rg[0], EXPERTS_PER_GROUP)])
    br = jnp.pad(br, (0, LANES - 2 * N_EXPERTS)).reshape(1, LANES)
    x1, t, comb = _out_proj(y_ssd, o, x, g_a, sc_f, sh_f, w_out[0].astype(BF16), ln1_g[0].reshape(1, d),
                            ln1_b[0].reshape(1, d), wr, br, alpha=alpha, tm=tm_in)

    by_group = lambda u: (u.astype(BF16).reshape(MOE_GROUPS, EXPERTS_PER_GROUP, d, D_EXPERT)
                          .transpose(0, 2, 1, 3).reshape(MOE_GROUPS, d, EXPERTS_PER_GROUP * D_EXPERT))
    wgu = jnp.concatenate([by_group(w_gate[0]), by_group(w_up[0])], axis=-1)
    wd = w_down[0].astype(BF16).reshape(MOE_GROUPS, EXPERTS_PER_GROUP * D_EXPERT, d)
    return _moe(t, comb, x1, g_f, wgu, wd, ln2_g[0].reshape(1, d), ln2_b[0].reshape(1, d), alpha=alpha, tm=tm_in)
```

```python
import functools
import math

import jax
import jax.numpy as jnp
from jax import lax
from jax.experimental import pallas as pl
from jax.experimental.pallas import tpu as pltpu

F32 = jnp.float32
BF16 = jnp.bfloat16

D_MODEL = 1024
GRID_W = 64
SSD_HEADS = 16
SSD_HEAD_DIM = 64
D_SSD = SSD_HEADS * SSD_HEAD_DIM
SSD_GROUPS = 2
SSD_HEADS_PER_GROUP = SSD_HEADS // SSD_GROUPS
D_STATE = 128
CONV_W = 5
CHUNK = 128
D_BC = SSD_GROUPS * D_STATE
D_XBC = D_SSD + 2 * D_BC
ATTN_Q_HEADS = 16
ATTN_KV_HEADS = 4
HEAD_DIM = 64
GQA_GROUP = ATTN_Q_HEADS // ATTN_KV_HEADS
D_ATTN = ATTN_Q_HEADS * HEAD_DIM
D_KV = ATTN_KV_HEADS * HEAD_DIM
ROPE_THETA = 10000.0
MOE_GROUPS = 4
EXPERTS_PER_GROUP = 4
N_EXPERTS = MOE_GROUPS * EXPERTS_PER_GROUP
D_EXPERT = 256

OFF_Q = D_SSD
OFF_XBC = OFF_Q + D_ATTN
OFF_DT = OFF_XBC + D_XBC
OFF_K = OFF_DT + 2 * SSD_HEADS
OFF_V = OFF_K + D_KV

LANES = 128
SUBLANES = 8
MXU_DIM = 256
VMEM_LIMIT = 56 * 1024 * 1024

DT_PAD = LANES
HALO = SUBLANES
NEG_BIG = -1e30


def _dot(a, b):
    return jnp.dot(a, b, preferred_element_type=F32)


def _dot_nt(a, b):
    return lax.dot_general(a, b, (((1,), (1,)), ((), ())), preferred_element_type=F32)


def _split2(a):
    hi = a.astype(BF16)
    lo = (a - hi.astype(F32)).astype(BF16)
    return hi, lo


def _split3(a):
    hi = a.astype(BF16)
    r = a - hi.astype(F32)
    mid = r.astype(BF16)
    lo = (r - mid.astype(F32)).astype(BF16)
    return hi, mid, lo


def _dot_f32(a, b):
    a_hi, a_lo = _split2(a)
    b_hi, b_lo = _split2(b)
    return _dot(a_hi, b_hi) + (_dot(a_hi, b_lo) + _dot(a_lo, b_hi))


def _silu(x):
    return x / (1.0 + jnp.exp(-x))


def _softplus(x):
    return jnp.maximum(x, 0.0) + jnp.log(1.0 + jnp.exp(-jnp.abs(x)))


def _layer_norm(x, g, b):
    mu = jnp.mean(x, -1, keepdims=True)
    xc = x - mu
    var = jnp.mean(xc * xc, -1, keepdims=True)
    return xc * lax.rsqrt(var + 1e-5) * g + b


def _params(*sem):
    return pltpu.CompilerParams(dimension_semantics=sem, vmem_limit_bytes=VMEM_LIMIT)


def _mod_kernel(c_ref, w_ref, b_ref, o_ref):
    o_ref[...] = _dot_f32(_silu(c_ref[...]), w_ref[...]) + b_ref[...]


def _modulation(c_rows, w_mod, b_mod):
    m, d = c_rows.shape
    n = w_mod.shape[1]
    tn = 512
    return pl.pallas_call(
        _mod_kernel,
        grid=(n // tn,),
        in_specs=[pl.BlockSpec((m, d), lambda j: (0, 0)),
                  pl.BlockSpec((d, tn), lambda j: (0, j)),
                  pl.BlockSpec((1, tn), lambda j: (0, j))],
        out_specs=pl.BlockSpec((m, tn), lambda j: (0, j)),
        out_shape=jax.ShapeDtypeStruct((m, n), F32),
        compiler_params=_params("parallel"),
        name="modulation",
    )(c_rows, w_mod, b_mod.reshape(1, n))


def _head_sumsq(x2, ones_bd):
    hi, lo = _split2(x2)
    blocks = []
    for j in range(x2.shape[1] // MXU_DIM):
        sl = slice(j * MXU_DIM, (j + 1) * MXU_DIM)
        blocks.append(_dot(hi[:, sl], ones_bd) + _dot(lo[:, sl], ones_bd))
    return blocks[0] if len(blocks) == 1 else jnp.concatenate(blocks, axis=1)


def _rope_partner(x):
    n = x.shape[1]
    half = HEAD_DIM // 2
    lane = lax.broadcasted_iota(jnp.int32, x.shape, 1)
    first_half = (lane & (HEAD_DIM - 1)) < half
    return jnp.where(first_half, pltpu.roll(x, n - half, 1), pltpu.roll(x, half, 1))


def _norm_rope(x, gain, ones_bd, cos, sin):
    ss = _head_sumsq(x * x, ones_bd)
    xn = x * lax.rsqrt(ss * (1.0 / HEAD_DIM) + 1e-6) * gain
    if cos is not None:
        reps = x.shape[1] // cos.shape[1]
        c = jnp.concatenate([cos] * reps, axis=1) if reps > 1 else cos
        s = jnp.concatenate([sin] * reps, axis=1) if reps > 1 else sin
        xn = xn * c + _rope_partner(xn) * s
    return xn


def _in_proj_kernel(*refs, with_zq, with_rope, tm):
    it = iter(refs)
    x_ref, xprev_ref, xnext_ref, sc_ref, sh_ref, w_ref, ones_ref, gq_ref, gk_ref, convw_ref, convb_ref = (
        next(it) for _ in range(11))
    cos_ref = sin_ref = None
    if with_rope:
        cos_ref, sin_ref = next(it), next(it)
    if with_zq:
        z_ref, q_ref = next(it), next(it)
    xbc_ref, k_ref, v_ref, dt_ref, pad_ref = (next(it) for _ in range(5))

    t = pl.program_id(0)
    scale, shift = 1.0 + sc_ref[0], sh_ref[0]
    hx = (x_ref[0] * scale + shift).astype(BF16)
    ones_bd = ones_ref[...]
    cos = cos_ref[...] if with_rope else None
    sin = sin_ref[...] if with_rope else None
    off = 0
    if with_zq:
        z_ref[0] = _dot(hx, w_ref[:, 0:D_SSD]).astype(BF16)
        q = _dot(hx, w_ref[:, D_SSD:D_SSD + D_ATTN])
        q_ref[0] = _norm_rope(q, gq_ref[...], ones_bd, cos, sin).astype(BF16)
        off = D_SSD + D_ATTN
    x_ext = jnp.concatenate([xprev_ref[0], x_ref[0], xnext_ref[0]], axis=0)
    raw = _dot((x_ext * scale + shift).astype(BF16), w_ref[:, off:off + D_XBC])
    row = lax.broadcasted_iota(jnp.int32, (tm + 2 * HALO, 1), 0)
    inside = ((row >= HALO) | (t > 0)) & ((row < tm + HALO) | (t < pl.num_programs(0) - 1))
    pad_ref[...] = jnp.where(inside, raw, 0.0)
    acc = convb_ref[...]
    for j in range(CONV_W):
        start = HALO - CONV_W // 2 + j
        acc = acc + convw_ref[j:j + 1, :] * pad_ref[start:start + tm, :]
    xbc_ref[0] = _silu(acc).astype(BF16)
    off += D_XBC
    k = _dot(hx, w_ref[:, off:off + D_KV])
    k_ref[0] = _norm_rope(k, gk_ref[...], ones_bd, cos, sin).astype(BF16)
    off += D_KV
    v_ref[0] = _dot(hx, w_ref[:, off:off + D_KV]).astype(BF16)
    off += D_KV
    dt_ref[0] = _dot(hx, w_ref[:, off:off + DT_PAD])


def _in_proj(x, scale, shift, w, ones_bd, gq, gk, convw, convb, cos, sin, *, with_zq, tm):
    b, l, d = x.shape
    nw = w.shape[1]
    with_rope = cos is not None
    per_batch = scale.shape[0] == b and b > 1
    mod_map = (lambda t, i: (i, 0, 0)) if per_batch else (lambda t, i: (0, 0, 0))
    row_map = lambda t, i: (i, t, 0)
    const2 = lambda t, i: (0, 0)
    per = tm // HALO
    in_specs = [pl.BlockSpec((1, tm, d), row_map),
                pl.BlockSpec((1, HALO, d), lambda t, i: (i, jnp.maximum(t * per - 1, 0), 0)),
                pl.BlockSpec((1, HALO, d), lambda t, i: (i, jnp.minimum((t + 1) * per, l // HALO - 1), 0)),
                pl.BlockSpec((1, 1, d), mod_map),
                pl.BlockSpec((1, 1, d), mod_map),
                pl.BlockSpec((d, nw), const2),
                pl.BlockSpec((MXU_DIM, MXU_DIM), const2),
                pl.BlockSpec((1, D_ATTN), const2),
                pl.BlockSpec((1, D_KV), const2),
                pl.BlockSpec((CONV_W, D_XBC), const2),
                pl.BlockSpec((1, D_XBC), const2)]
    args = [x, x, x, scale, shift, w, ones_bd, gq, gk, convw, convb]
    if with_rope:
        in_specs += [pl.BlockSpec((tm, cos.shape[1]), lambda t, i: (t, 0))] * 2
        args += [cos, sin]
    out_shape, out_specs = [], []

    def add_out(width, dtype):
        out_shape.append(jax.ShapeDtypeStruct((b, l, width), dtype))
        out_specs.append(pl.BlockSpec((1, tm, width), row_map))

    if with_zq:
        add_out(D_SSD, BF16)
        add_out(D_ATTN, BF16)
    add_out(D_XBC, BF16)
    add_out(D_KV, BF16)
    add_out(D_KV, BF16)
    add_out(DT_PAD, F32)
    return pl.pallas_call(
        functools.partial(_in_proj_kernel, with_zq=with_zq, with_rope=with_rope, tm=tm),
        grid=(l // tm, b),
        in_specs=in_specs,
        out_specs=out_specs,
        out_shape=out_shape,
        scratch_shapes=[pltpu.VMEM((tm + 2 * HALO, D_XBC), F32)],
        compiler_params=_params("parallel", "parallel"),
        name="in_proj_latent" if with_zq else "in_proj_ctx",
    )(*args)


PAIR = 2 * SSD_HEAD_DIM
PAIRS_PER_GROUP = SSD_HEADS_PER_GROUP // 2
D_GROUP = SSD_HEADS_PER_GROUP * SSD_HEAD_DIM


def _chunk_tables(dt_raw, dtb, a_neg, direction):
    dtv = _softplus(dt_raw + dtb)
    qi = lax.broadcasted_iota(jnp.int32, (CHUNK, CHUNK), 0)
    si = lax.broadcasted_iota(jnp.int32, (CHUNK, CHUNK), 1)
    mask = (si <= qi) if direction == 0 else (si >= qi)
    tri = jnp.where(mask, 1.0, 0.0).astype(BF16)
    hi, mid, lo = _split3(dtv * a_neg)
    a = _dot(tri, hi) + (_dot(tri, mid) + _dot(tri, lo))
    a_t = a.T
    dtv_t = dtv.T
    last = CHUNK - 1 if direction == 0 else 0
    tot = a_t[:, last:last + 1]
    return dict(a=a, mask=mask, rowp_t=a_t - jnp.log(dtv_t), f_t=dtv_t * jnp.exp(tot - a_t), etot_t=jnp.exp(tot))


def _ssd_chunk_dir(act, tables, direction, h_ref, want_y):
    a, mask, rowp_t, f_t, etot_t = (tables[k] for k in ("a", "mask", "rowp_t", "f_t", "etot_t"))
    lane = lax.broadcasted_iota(jnp.int32, (CHUNK, PAIR), 1)
    low = lane < SSD_HEAD_DIM
    zero = jnp.zeros((CHUNK, PAIR), BF16)
    ys = []
    for g in range(SSD_GROUPS):
        bg = act[:, D_SSD + g * D_STATE:D_SSD + (g + 1) * D_STATE]
        bg_t = bg.astype(F32).T
        if want_y:
            cg = act[:, D_SSD + D_BC + g * D_STATE:D_SSD + D_BC + (g + 1) * D_STATE]
            cb = _dot_nt(cg, bg)
            y_off = _dot(cg, h_ref[g].astype(BF16))
        for j in range(PAIRS_PER_GROUP):
            col0 = (g * PAIRS_PER_GROUP + j) * PAIR
            xp = act[:, col0:col0 + PAIR]
            x_bd = jnp.concatenate([jnp.where(low, xp, zero), jnp.where(low, zero, xp)], axis=0)
            m_parts, s_parts, e_parts, etots = [], [], [], []
            for u in range(2):
                hc = direction * SSD_HEADS + g * SSD_HEADS_PER_GROUP + 2 * j + u
                s_parts.append((bg_t * f_t[hc:hc + 1, :]).astype(BF16))
                etots.append(jnp.broadcast_to(etot_t[hc:hc + 1, :], (D_STATE, PAIR)))
                if want_y:
                    col = jnp.broadcast_to(a[:, hc:hc + 1], (CHUNK, CHUNK))
                    decay = jnp.exp(jnp.where(mask, col - rowp_t[hc:hc + 1, :], NEG_BIG))
                    m_parts.append((cb * decay).astype(BF16))
                    e_parts.append(jnp.exp(col))
            hs = slice(j * PAIR, (j + 1) * PAIR)
            new_state = jnp.where(low, etots[0], etots[1]) * h_ref[g, :, hs] + _dot(
                jnp.concatenate(s_parts, axis=1), x_bd)
            if want_y:
                y_diag = _dot(jnp.concatenate(m_parts, axis=1), x_bd)
                ys.append(y_diag + y_off[:, hs] * jnp.where(low, e_parts[0], e_parts[1]))
            h_ref[g, :, hs] = new_state
    return jnp.concatenate(ys, axis=1) if want_y else None


def _ssd_ctx_kernel(act_ref, dt_ref, dtb_ref, a_ref, hout_ref, h_ref, *, n_chunks):
    dtb, a_neg = dtb_ref[...], a_ref[...]
    for direction in range(2):
        h_ref[...] = jnp.zeros(h_ref.shape, F32)
        order = range(n_chunks) if direction == 0 else range(n_chunks - 1, -1, -1)
        for c in order:
            rows = slice(c * CHUNK, (c + 1) * CHUNK)
            tables = _chunk_tables(dt_ref[0, rows, :], dtb, a_neg, direction)
            _ssd_chunk_dir(act_ref[0, rows, :], tables, direction, h_ref, False)
        hout_ref[0, direction] = h_ref[...]


def _ssd_ctx(act, dt, dtb, a_neg):
    b, s, _ = act.shape
    n_chunks = s // CHUNK
    const2 = lambda i: (0, 0)
    state = (SSD_GROUPS, D_STATE, D_GROUP)
    return pl.pallas_call(
        functools.partial(_ssd_ctx_kernel, n_chunks=n_chunks),
        grid=(b,),
        in_specs=[pl.BlockSpec((1, s, D_XBC), lambda i: (i, 0, 0)),
                  pl.BlockSpec((1, s, DT_PAD), lambda i: (i, 0, 0)),
                  pl.BlockSpec((1, DT_PAD), const2),
                  pl.BlockSpec((1, DT_PAD), const2)],
        out_specs=pl.BlockSpec((1, 2) + state, lambda i: (i, 0, 0, 0, 0)),
        out_shape=jax.ShapeDtypeStruct((b, 2) + state, F32),
        scratch_shapes=[pltpu.VMEM(state, F32)],
        compiler_params=_params("parallel"),
        name="ssd_ctx",
    )(act, dt, dtb, a_neg)


def _ssd_kernel(act_ref, dt_ref, z_ref, h0_ref, dtb_ref, a_ref, skip_ref, ng_ref, out_ref, y_ref, h_ref, *, n_chunks):
    j = pl.program_id(1)
    dtb, a_neg = dtb_ref[...], a_ref[...]

    @pl.when(j < n_chunks)
    def _forward():
        @pl.when(j == 0)
        def _():
            h_ref[...] = h0_ref[0, 0]

        act = act_ref[0]
        y = _ssd_chunk_dir(act, _chunk_tables(dt_ref[0], dtb, a_neg, 0), 0, h_ref, True)
        rows = pl.ds(pl.multiple_of(j * CHUNK, CHUNK), CHUNK)
        y_ref[rows, :] = y + act[:, 0:D_SSD].astype(F32) * skip_ref[...]

    @pl.when(j >= n_chunks)
    def _backward():
        @pl.when(j == n_chunks)
        def _():
            h_ref[...] = h0_ref[0, 1]

        c = 2 * n_chunks - 1 - j
        rows = pl.ds(pl.multiple_of(c * CHUNK, CHUNK), CHUNK)
        y = y_ref[rows, :] + _ssd_chunk_dir(act_ref[0], _chunk_tables(dt_ref[0], dtb, a_neg, 1), 1, h_ref, True)
        y = y * _silu(z_ref[0].astype(F32))
        parts = []
        for g in range(SSD_GROUPS):
            yg = y[:, g * D_GROUP:(g + 1) * D_GROUP]
            parts.append(yg * lax.rsqrt(jnp.mean(yg * yg, -1, keepdims=True) + 1e-6))
        out_ref[0] = (jnp.concatenate(parts, axis=1) * ng_ref[...]).astype(BF16)


def _ssd(act, dt, z, h0, dtb, a_neg, skip, ng):
    b, l, _ = act.shape
    nc = l // CHUNK

    def chunk_of(j):
        return jnp.where(j < nc, j, 2 * nc - 1 - j)

    def bwd_chunk(j):
        return jnp.where(j < nc, nc - 1, 2 * nc - 1 - j)

    const2 = lambda i, j: (0, 0)
    state = (SSD_GROUPS, D_STATE, D_GROUP)
    return pl.pallas_call(
        functools.partial(_ssd_kernel, n_chunks=nc),
        grid=(b, 2 * nc),
        in_specs=[pl.BlockSpec((1, CHUNK, D_XBC), lambda i, j: (i, chunk_of(j), 0)),
                  pl.BlockSpec((1, CHUNK, DT_PAD), lambda i, j: (i, chunk_of(j), 0)),
                  pl.BlockSpec((1, CHUNK, D_SSD), lambda i, j: (i, bwd_chunk(j), 0)),
                  pl.BlockSpec((1, 2) + state, lambda i, j: (i, 0, 0, 0, 0)),
                  pl.BlockSpec((1, DT_PAD), const2),
                  pl.BlockSpec((1, DT_PAD), const2),
                  pl.BlockSpec((1, D_SSD), const2),
                  pl.BlockSpec((1, D_SSD), const2)],
        out_specs=pl.BlockSpec((1, CHUNK, D_SSD), lambda i, j: (i, bwd_chunk(j), 0)),
        out_shape=jax.ShapeDtypeStruct((b, l, D_SSD), BF16),
        scratch_shapes=[pltpu.VMEM((l, D_SSD), F32),
                        pltpu.VMEM(state, F32)],
        compiler_params=_params("parallel", "arbitrary"),
        name="ssd",
    )(act, dt, z, h0, dtb, a_neg, skip, ng)


ATTN_ROWS = 128


def _attn_kernel(q_ref, k_ref, v_ref, o_ref, *, tq):
    for r in range(tq // ATTN_ROWS):
        rows = slice(r * ATTN_ROWS, (r + 1) * ATTN_ROWS)
        for kvh in range(ATTN_KV_HEADS):
            base = kvh * GQA_GROUP * HEAD_DIM
            q = jnp.concatenate(
                [q_ref[0, rows, base + g * HEAD_DIM:base + (g + 1) * HEAD_DIM] for g in range(GQA_GROUP)], axis=0)
            s = _dot_nt(q, k_ref[0, kvh])
            p = jnp.exp2(s - jnp.max(s, -1, keepdims=True))
            pv = _dot(p.astype(BF16), v_ref[0, kvh])
            o = pv[:, 0:HEAD_DIM] / pv[:, HEAD_DIM:HEAD_DIM + 1]
            o_ref[0, rows, base:base + GQA_GROUP * HEAD_DIM] = jnp.concatenate(
                [o[g * ATTN_ROWS:(g + 1) * ATTN_ROWS] for g in range(GQA_GROUP)], axis=1).astype(BF16)


def _attention(q, k, v, *, tq):
    b, l, _ = q.shape
    s = k.shape[2]
    return pl.pallas_call(
        functools.partial(_attn_kernel, tq=tq),
        grid=(b, l // tq),
        in_specs=[pl.BlockSpec((1, tq, D_ATTN), lambda i, t: (i, t, 0)),
                  pl.BlockSpec((1, ATTN_KV_HEADS, s, HEAD_DIM), lambda i, t: (i, 0, 0, 0)),
                  pl.BlockSpec((1, ATTN_KV_HEADS, s, LANES), lambda i, t: (i, 0, 0, 0))],
        out_specs=pl.BlockSpec((1, tq, D_ATTN), lambda i, t: (i, t, 0)),
        out_shape=jax.ShapeDtypeStruct((b, l, D_ATTN), BF16),
        compiler_params=_params("parallel", "parallel"),
        name="attention",
    )(q, k, v)


def _first_index_of_max(vals, lane):
    m = jnp.max(vals, -1, keepdims=True)
    idx = jnp.min(jnp.where(vals == m, lane, LANES), -1, keepdims=True)
    return m, idx


def _router(logits):
    lane = lax.broadcasted_iota(jnp.int32, logits.shape, 1)
    is_e = lane < N_EXPERTS
    is_g = (lane >= N_EXPERTS) & (lane < 2 * N_EXPERTS)
    gl = jnp.where(is_g, logits, NEG_BIG)
    g_max, g_lane = _first_index_of_max(gl, lane)
    g_sum = jnp.sum(jnp.where(is_g, jnp.exp(gl - g_max), 0.0), -1, keepdims=True) * (1.0 / EXPERTS_PER_GROUP)
    g_val = 1.0 / g_sum
    g_idx = (g_lane - N_EXPERTS) // EXPERTS_PER_GROUP
    in_group = is_e & ((lane // EXPERTS_PER_GROUP) == g_idx)
    el = jnp.where(in_group, logits, NEG_BIG)
    m1, i1 = _first_index_of_max(el, lane)
    el2 = jnp.where(lane == i1, NEG_BIG, el)
    m2, i2 = _first_index_of_max(el2, lane)
    r = jnp.exp(m2 - m1)
    w1 = 1.0 / (1.0 + r)
    w2 = r * w1
    return g_val * (jnp.where(lane == i1, w1, 0.0) + jnp.where(lane == i2, w2, 0.0))


def _out_proj_kernel(y_ref, o_ref, x_ref, ga_ref, scf_ref, shf_ref, w_ref, g1_ref, b1_ref, wr_ref, br_ref,
                     x1_ref, t_ref, comb_ref, *, alpha):
    mix = _dot(y_ref[0], w_ref[0:D_SSD, :]) + _dot(o_ref[0], w_ref[D_SSD:D_SSD + D_ATTN, :])
    x1 = _layer_norm(alpha * x_ref[0] + ga_ref[0] * mix, g1_ref[...], b1_ref[...])
    x1_ref[0] = x1
    t = x1 * (1.0 + scf_ref[0]) + shf_ref[0]
    t_ref[0] = t.astype(BF16)
    comb_ref[0] = _router(_dot_f32(t, wr_ref[...]) + br_ref[...])


def _out_proj(y, o, x, ga, scf, shf, w_out, g1, b1, wr, br, *, alpha, tm):
    b, l, d = x.shape
    row_map = lambda i, t: (i, t, 0)
    mod_map = lambda i, t: (i, 0, 0)
    const2 = lambda i, t: (0, 0)
    return pl.pallas_call(
        functools.partial(_out_proj_kernel, alpha=alpha),
        grid=(b, l // tm),
        in_specs=[pl.BlockSpec((1, tm, D_SSD), row_map),
                  pl.BlockSpec((1, tm, D_ATTN), row_map),
                  pl.BlockSpec((1, tm, d), row_map),
                  pl.BlockSpec((1, 1, d), mod_map),
                  pl.BlockSpec((1, 1, d), mod_map),
                  pl.BlockSpec((1, 1, d), mod_map),
                  pl.BlockSpec((D_SSD + D_ATTN, d), const2),
                  pl.BlockSpec((1, d), const2),
                  pl.BlockSpec((1, d), const2),
                  pl.BlockSpec((d, LANES), const2),
                  pl.BlockSpec((1, LANES), const2)],
        out_specs=[pl.BlockSpec((1, tm, d), row_map),
                   pl.BlockSpec((1, tm, d), row_map),
                   pl.BlockSpec((1, tm, LANES), row_map)],
        out_shape=[jax.ShapeDtypeStruct((b, l, d), F32),
                   jax.ShapeDtypeStruct((b, l, d), BF16),
                   jax.ShapeDtypeStruct((b, l, LANES), F32)],
        compiler_params=_params("parallel", "parallel"),
        name="out_proj",
    )(y, o, x, ga, scf, shf, w_out, g1, b1, wr, br)


def _moe_kernel(t_ref, comb_ref, x1_ref, gf_ref, wgu_ref, wd_ref, g2_ref, b2_ref, out_ref, acc_ref, *, alpha):
    t = t_ref[0]
    comb = comb_ref[0]
    lane = lax.broadcasted_iota(jnp.int32, comb.shape, 1)
    width = EXPERTS_PER_GROUP * D_EXPERT

    def group_out(g):
        gu = _dot(t, wgu_ref[g])
        weights = []
        for i in range(EXPERTS_PER_GROUP):
            ce = jnp.sum(jnp.where(lane == g * EXPERTS_PER_GROUP + i, comb, 0.0), -1, keepdims=True)
            weights.append(jnp.broadcast_to(ce, (comb.shape[0], D_EXPERT)))
        hid = _silu(gu[:, 0:width]) * gu[:, width:2 * width] * jnp.concatenate(weights, axis=1)
        return _dot(hid.astype(BF16), wd_ref[g])

    acc_ref[...] = group_out(0)

    def body(g, carry):
        acc_ref[...] += group_out(g)
        return carry

    lax.fori_loop(1, MOE_GROUPS, body, 0)
    out_ref[0] = _layer_norm(alpha * x1_ref[0] + gf_ref[0] * acc_ref[...], g2_ref[...], b2_ref[...])


def _moe(t, comb, x1, gf, wgu, wd, g2, b2, *, alpha, tm):
    b, l, d = x1.shape
    row_map = lambda i, r: (i, r, 0)
    const2 = lambda i, r: (0, 0)
    const3 = lambda i, r: (0, 0, 0)
    resident = pl.Buffered(1)
    return pl.pallas_call(
        functools.partial(_moe_kernel, alpha=alpha),
        grid=(b, l // tm),
        in_specs=[pl.BlockSpec((1, tm, d), row_map),
                  pl.BlockSpec((1, tm, LANES), row_map),
                  pl.BlockSpec((1, tm, d), row_map),
                  pl.BlockSpec((1, 1, d), lambda i, r: (i, 0, 0)),
                  pl.BlockSpec(wgu.shape, const3, pipeline_mode=resident),
                  pl.BlockSpec(wd.shape, const3, pipeline_mode=resident),
                  pl.BlockSpec((1, d), const2),
                  pl.BlockSpec((1, d), const2)],
        out_specs=pl.BlockSpec((1, tm, d), row_map),
        out_shape=jax.ShapeDtypeStruct((b, l, d), F32),
        scratch_shapes=[pltpu.VMEM((tm, d), F32)],
        compiler_params=_params("parallel", "parallel"),
        name="moe",
    )(t, comb, x1, gf, wgu, wd, g2, b2)


def _rope_tables(n_rows):
    rows = jnp.broadcast_to(jnp.arange(n_rows, dtype=F32)[:, None], (n_rows, GRID_W)).reshape(-1)
    cols = jnp.broadcast_to(jnp.arange(GRID_W, dtype=F32)[None, :], (n_rows, GRID_W)).reshape(-1)
    axis_dim = HEAD_DIM // 2
    inv_freq = jnp.power(ROPE_THETA, -jnp.arange(0, axis_dim, 2, dtype=F32) / axis_dim)
    ang = jnp.concatenate([rows[:, None] * inv_freq, cols[:, None] * inv_freq], -1)
    cos, sin = jnp.cos(ang), jnp.sin(ang)
    cos_h = jnp.concatenate([cos, cos], -1)
    sin_h = jnp.concatenate([-sin, sin], -1)
    reps = LANES // HEAD_DIM
    return jnp.tile(cos_h, (1, reps)), jnp.tile(sin_h, (1, reps))


def _head_perm(n_heads):
    within = jnp.concatenate([jnp.arange(0, HEAD_DIM, 2), jnp.arange(1, HEAD_DIM, 2)])
    return (jnp.arange(n_heads)[:, None] * HEAD_DIM + within[None, :]).reshape(-1)


def kernel(x, c, ctx, c_ctx, w_mod, b_mod, w_in, conv_w, conv_b, dt_bias, a_log, d_skip, ssd_norm_g, q_norm_g,
           k_norm_g, w_out, ln1_g, ln1_b, w_rg, b_rg, w_re, b_re, w_gate, w_up, w_down, ln2_g, ln2_b):
    depth = w_in.shape[0]
    assert depth == 1, "single-layer block only"
    b, l, d = x.shape
    s_ctx = ctx.shape[1]
    assert d == D_MODEL and l % CHUNK == 0 and s_ctx % CHUNK == 0 and l % GRID_W == 0
    alpha = (2.0 * depth) ** 0.25

    m_rows = -(-(b + 1) // SUBLANES) * SUBLANES
    c_rows = jnp.zeros((m_rows, d), F32).at[:b].set(c).at[b].set(c_ctx)
    mod = _modulation(c_rows, w_mod[0], b_mod[0])
    sh_a, sc_a, g_a, sh_f, sc_f, g_f = [mod[:b, i * d:(i + 1) * d].reshape(b, 1, d) for i in range(6)]
    sh_c = mod[b:b + 1, 0:d].reshape(1, 1, d)
    sc_c = mod[b:b + 1, d:2 * d].reshape(1, 1, d)

    w = w_in[0]
    perm_q, perm_k = _head_perm(ATTN_Q_HEADS), _head_perm(ATTN_KV_HEADS)
    w_dt = jnp.pad(w[:, OFF_DT:OFF_K], ((0, 0), (0, DT_PAD - 2 * SSD_HEADS)))
    w_side = jnp.concatenate([w[:, OFF_XBC:OFF_DT], w[:, OFF_K:OFF_V][:, perm_k], w[:, OFF_V:], w_dt], axis=1)
    w_full = jnp.concatenate([w[:, :OFF_Q], w[:, OFF_Q:OFF_XBC][:, perm_q], w_side], axis=1).astype(BF16)
    w_side = w_side.astype(BF16)
    within = perm_k[:HEAD_DIM]
    gq = jnp.tile(q_norm_g[0][within], ATTN_Q_HEADS).reshape(1, D_ATTN) * (HEAD_DIM ** -0.5 * math.log2(math.e))
    gk = jnp.tile(k_norm_g[0][within], ATTN_KV_HEADS).reshape(1, D_KV)
    lane = jnp.arange(MXU_DIM)
    ones_bd = (lane[:, None] // HEAD_DIM == lane[None, :] // HEAD_DIM).astype(BF16)
    cos, sin = _rope_tables(l // GRID_W)

    convw, convb = conv_w[0], conv_b[0].reshape(1, D_XBC)
    tm_in = 512 if l % 512 == 0 else CHUNK
    act_c, k_c, v_c, dt_c = _in_proj(ctx, sc_c, sh_c, w_side, ones_bd, gq, gk, convw, convb, None, None,
                                     with_zq=False, tm=s_ctx)
    z, q, act, k, v, dt = _in_proj(x, sc_a, sh_a, w_full, ones_bd, gq, gk, convw, convb, cos, sin,
                                   with_zq=True, tm=tm_in)

    pad_dt = lambda u: jnp.pad(u.reshape(1, 2 * SSD_HEADS).astype(F32), ((0, 0), (0, DT_PAD - 2 * SSD_HEADS)))
    dtb = pad_dt(dt_bias[0])
    a_neg = pad_dt(-jnp.exp(a_log[0].astype(F32)))
    skip = jnp.repeat(d_skip[0].astype(F32), SSD_HEAD_DIM).reshape(1, D_SSD)
    ng = ssd_norm_g[0].reshape(1, D_SSD)
    h_ctx = _ssd_ctx(act_c, dt_c, dtb, a_neg)
    y_ssd = _ssd(act, dt, z, h_ctx, dtb, a_neg, skip, ng)

    to_heads = lambda u: u.reshape(b, -1, ATTN_KV_HEADS, HEAD_DIM).transpose(0, 2, 1, 3)
    k_all = to_heads(jnp.concatenate([k_c, k], axis=1))
    v_all = to_heads(jnp.concatenate([v_c, v], axis=1))
    ones_col = (jnp.arange(LANES - HEAD_DIM) == 0).astype(BF16)
    v_all = jnp.concatenate([v_all, jnp.broadcast_to(ones_col, v_all.shape[:-1] + (LANES - HEAD_DIM,))], axis=-1)
    o = _attention(q, k_all, v_all, tq=512 if l % 512 == 0 else ATTN_ROWS)

    wr = jnp.concatenate([w_re[0], jnp.repeat(w_rg[0], EXPERTS_PER_GROUP, axis=1)], axis=1)
    wr = jnp.pad(wr, ((0, 0), (0, LANES - 2 * N_EXPERTS)))
    br = jnp.concatenate([b_re[0], jnp.repeat(b_rg[0], EXPERTS_PER_GROUP)])
    br = jnp.pad(br, (0, LANES - 2 * N_EXPERTS)).reshape(1, LANES)
    x1, t, comb = _out_proj(y_ssd, o, x, g_a, sc_f, sh_f, w_out[0].astype(BF16), ln1_g[0].reshape(1, d),
                            ln1_b[0].reshape(1, d), wr, br, alpha=alpha, tm=tm_in)

    by_group = lambda u: (u.astype(BF16).reshape(MOE_GROUPS, EXPERTS_PER_GROUP, d, D_EXPERT)
                          .transpose(0, 2, 1, 3).reshape(MOE_GROUPS, d, EXPERTS_PER_GROUP * D_EXPERT))
    wgu = jnp.concatenate([by_group(w_gate[0]), by_group(w_up[0])], axis=-1)
    wd = w_down[0].astype(BF16).reshape(MOE_GROUPS, EXPERTS_PER_GROUP * D_EXPERT, d)
    return _moe(t, comb, x1, g_f, wgu, wd, ln2_g[0].reshape(1, d), ln2_b[0].reshape(1, d), alpha=alpha, tm=tm_in)
```

```python
import functools
import math

import jax
import jax.numpy as jnp
from jax import lax
from jax.experimental import pallas as pl
from jax.experimental.pallas import tpu as pltpu

F32 = jnp.float32
BF16 = jnp.bfloat16

D_MODEL = 1024
GRID_W = 64
SSD_HEADS = 16
SSD_HEAD_DIM = 64
D_SSD = SSD_HEADS * SSD_HEAD_DIM
SSD_GROUPS = 2
SSD_HEADS_PER_GROUP = SSD_HEADS // SSD_GROUPS
D_STATE = 128
CONV_W = 5
CHUNK = 128
D_BC = SSD_GROUPS * D_STATE
D_XBC = D_SSD + 2 * D_BC
ATTN_Q_HEADS = 16
ATTN_KV_HEADS = 4
HEAD_DIM = 64
GQA_GROUP = ATTN_Q_HEADS // ATTN_KV_HEADS
D_ATTN = ATTN_Q_HEADS * HEAD_DIM
D_KV = ATTN_KV_HEADS * HEAD_DIM
ROPE_THETA = 10000.0
MOE_GROUPS = 4
EXPERTS_PER_GROUP = 4
N_EXPERTS = MOE_GROUPS * EXPERTS_PER_GROUP
D_EXPERT = 256

OFF_Q = D_SSD
OFF_XBC = OFF_Q + D_ATTN
OFF_DT = OFF_XBC + D_XBC
OFF_K = OFF_DT + 2 * SSD_HEADS
OFF_V = OFF_K + D_KV

LANES = 128
SUBLANES = 8
MXU_DIM = 256
VMEM_LIMIT = 56 * 1024 * 1024

DT_PAD = LANES
HALO = SUBLANES
NEG_BIG = -1e30


def _dot(a, b):
    return jnp.dot(a, b, preferred_element_type=F32)


def _dot_nt(a, b):
    return lax.dot_general(a, b, (((1,), (1,)), ((), ())), preferred_element_type=F32)


def _split2(a):
    hi = a.astype(BF16)
    lo = (a - hi.astype(F32)).astype(BF16)
    return hi, lo


def _split3(a):
    hi = a.astype(BF16)
    r = a - hi.astype(F32)
    mid = r.astype(BF16)
    lo = (r - mid.astype(F32)).astype(BF16)
    return hi, mid, lo


def _dot_f32(a, b):
    a_hi, a_lo = _split2(a)
    b_hi, b_lo = _split2(b)
    return _dot(a_hi, b_hi) + (_dot(a_hi, b_lo) + _dot(a_lo, b_hi))


def _silu(x):
    return x / (1.0 + jnp.exp(-x))


def _softplus(x):
    return jnp.maximum(x, 0.0) + jnp.log(1.0 + jnp.exp(-jnp.abs(x)))


def _layer_norm(x, g, b):
    mu = jnp.mean(x, -1, keepdims=True)
    xc = x - mu
    var = jnp.mean(xc * xc, -1, keepdims=True)
    return xc * lax.rsqrt(var + 1e-5) * g + b


def _params(*sem):
    return pltpu.CompilerParams(dimension_semantics=sem, vmem_limit_bytes=VMEM_LIMIT)


def _mod_kernel(c_ref, w_ref, b_ref, o_ref):
    o_ref[...] = _dot_f32(_silu(c_ref[...]), w_ref[...]) + b_ref[...]


def _modulation(c_rows, w_mod, b_mod):
    m, d = c_rows.shape
    n = w_mod.shape[1]
    tn = 512
    return pl.pallas_call(
        _mod_kernel,
        grid=(n // tn,),
        in_specs=[pl.BlockSpec((m, d), lambda j: (0, 0)),
                  pl.BlockSpec((d, tn), lambda j: (0, j)),
                  pl.BlockSpec((1, tn), lambda j: (0, j))],
        out_specs=pl.BlockSpec((m, tn), lambda j: (0, j)),
        out_shape=jax.ShapeDtypeStruct((m, n), F32),
        compiler_params=_params("parallel"),
        name="modulation",
    )(c_rows, w_mod, b_mod.reshape(1, n))


def _head_sumsq(x2, ones_bd):
    hi, lo = _split2(x2)
    blocks = []
    for j in range(x2.shape[1] // MXU_DIM):
        sl = slice(j * MXU_DIM, (j + 1) * MXU_DIM)
        blocks.append(_dot(hi[:, sl], ones_bd) + _dot(lo[:, sl], ones_bd))
    return blocks[0] if len(blocks) == 1 else jnp.concatenate(blocks, axis=1)


def _rope_partner(x):
    n = x.shape[1]
    half = HEAD_DIM // 2
    lane = lax.broadcasted_iota(jnp.int32, x.shape, 1)
    first_half = (lane & (HEAD_DIM - 1)) < half
    return jnp.where(first_half, pltpu.roll(x, n - half, 1), pltpu.roll(x, half, 1))


def _norm_rope(x, gain, ones_bd, cos, sin):
    ss = _head_sumsq(x * x, ones_bd)
    xn = x * lax.rsqrt(ss * (1.0 / HEAD_DIM) + 1e-6) * gain
    if cos is not None:
        reps = x.shape[1] // cos.shape[1]
        c = jnp.concatenate([cos] * reps, axis=1) if reps > 1 else cos
        s = jnp.concatenate([sin] * reps, axis=1) if reps > 1 else sin
        xn = xn * c + _rope_partner(xn) * s
    return xn


def _in_proj_kernel(*refs, with_zq, with_rope, tm):
    it = iter(refs)
    x_ref, xprev_ref, xnext_ref, sc_ref, sh_ref, w_ref, ones_ref, gq_ref, gk_ref, convw_ref, convb_ref = (
        next(it) for _ in range(11))
    cos_ref = sin_ref = None
    if with_rope:
        cos_ref, sin_ref = next(it), next(it)
    if with_zq:
        z_ref, q_ref = next(it), next(it)
    xbc_ref, k_ref, v_ref, dt_ref, pad_ref = (next(it) for _ in range(5))

    t = pl.program_id(0)
    scale, shift = 1.0 + sc_ref[0], sh_ref[0]
    hx = (x_ref[0] * scale + shift).astype(BF16)
    ones_bd = ones_ref[...]
    cos = cos_ref[...] if with_rope else None
    sin = sin_ref[...] if with_rope else None
    off = 0
    if with_zq:
        z_ref[0] = _dot(hx, w_ref[:, 0:D_SSD]).astype(BF16)
        q = _dot(hx, w_ref[:, D_SSD:D_SSD + D_ATTN])
        q_ref[0] = _norm_rope(q, gq_ref[...], ones_bd, cos, sin).astype(BF16)
        off = D_SSD + D_ATTN
    x_ext = jnp.concatenate([xprev_ref[0], x_ref[0], xnext_ref[0]], axis=0)
    raw = _dot((x_ext * scale + shift).astype(BF16), w_ref[:, off:off + D_XBC])
    row = lax.broadcasted_iota(jnp.int32, (tm + 2 * HALO, 1), 0)
    inside = ((row >= HALO) | (t > 0)) & ((row < tm + HALO) | (t < pl.num_programs(0) - 1))
    pad_ref[...] = jnp.where(inside, raw, 0.0)
    acc = convb_ref[...]
    for j in range(CONV_W):
        start = HALO - CONV_W // 2 + j
        acc = acc + convw_ref[j:j + 1, :] * pad_ref[start:start + tm, :]
    xbc_ref[0] = _silu(acc).astype(BF16)
    off += D_XBC
    k = _dot(hx, w_ref[:, off:off + D_KV])
    k_ref[0] = _norm_rope(k, gk_ref[...], ones_bd, cos, sin).astype(BF16)
    off += D_KV
    v_ref[0] = _dot(hx, w_ref[:, off:off + D_KV]).astype(BF16)
    off += D_KV
    dt_ref[0] = _dot(hx, w_ref[:, off:off + DT_PAD])


def _in_proj(x, scale, shift, w, ones_bd, gq, gk, convw, convb, cos, sin, *, with_zq, tm):
    b, l, d = x.shape
    nw = w.shape[1]
    with_rope = cos is not None
    per_batch = scale.shape[0] == b and b > 1
    mod_map = (lambda t, i: (i, 0, 0)) if per_batch else (lambda t, i: (0, 0, 0))
    row_map = lambda t, i: (i, t, 0)
    const2 = lambda t, i: (0, 0)
    per = tm // HALO
    in_specs = [pl.BlockSpec((1, tm, d), row_map),
                pl.BlockSpec((1, HALO, d), lambda t, i: (i, jnp.maximum(t * per - 1, 0), 0)),
                pl.BlockSpec((1, HALO, d), lambda t, i: (i, jnp.minimum((t + 1) * per, l // HALO - 1), 0)),
                pl.BlockSpec((1, 1, d), mod_map),
                pl.BlockSpec((1, 1, d), mod_map),
                pl.BlockSpec((d, nw), const2),
                pl.BlockSpec((MXU_DIM, MXU_DIM), const2),
                pl.BlockSpec((1, D_ATTN), const2),
                pl.BlockSpec((1, D_KV), const2),
                pl.BlockSpec((CONV_W, D_XBC), const2),
                pl.BlockSpec((1, D_XBC), const2)]
    args = [x, x, x, scale, shift, w, ones_bd, gq, gk, convw, convb]
    if with_rope:
        in_specs += [pl.BlockSpec((tm, cos.shape[1]), lambda t, i: (t, 0))] * 2
        args += [cos, sin]
    out_shape, out_specs = [], []

    def add_out(width, dtype):
        out_shape.append(jax.ShapeDtypeStruct((b, l, width), dtype))
        out_specs.append(pl.BlockSpec((1, tm, width), row_map))

    if with_zq:
        add_out(D_SSD, BF16)
        add_out(D_ATTN, BF16)
    add_out(D_XBC, BF16)
    add_out(D_KV, BF16)
    add_out(D_KV, BF16)
    add_out(DT_PAD, F32)
    return pl.pallas_call(
        functools.partial(_in_proj_kernel, with_zq=with_zq, with_rope=with_rope, tm=tm),
        grid=(l // tm, b),
        in_specs=in_specs,
        out_specs=out_specs,
        out_shape=out_shape,
        scratch_shapes=[pltpu.VMEM((tm + 2 * HALO, D_XBC), F32)],
        compiler_params=_params("parallel", "parallel"),
        name="in_proj_latent" if with_zq else "in_proj_ctx",
    )(*args)


PAIR = 2 * SSD_HEAD_DIM
PAIRS_PER_GROUP = SSD_HEADS_PER_GROUP // 2
D_GROUP = SSD_HEADS_PER_GROUP * SSD_HEAD_DIM


def _chunk_tables(dt_raw, dtb, a_neg, direction):
    dtv = _softplus(dt_raw + dtb)
    qi = lax.broadcasted_iota(jnp.int32, (CHUNK, CHUNK), 0)
    si = lax.broadcasted_iota(jnp.int32, (CHUNK, CHUNK), 1)
    mask = (si <= qi) if direction == 0 else (si >= qi)
    tri = jnp.where(mask, 1.0, 0.0).astype(BF16)
    hi, mid, lo = _split3(dtv * a_neg)
    a = _dot(tri, hi) + (_dot(tri, mid) + _dot(tri, lo))
    a_t = a.T
    dtv_t = dtv.T
    last = CHUNK - 1 if direction == 0 else 0
    tot = a_t[:, last:last + 1]
    return dict(a=a, mask=mask, rowp_t=a_t - jnp.log(dtv_t), f_t=dtv_t * jnp.exp(tot - a_t), etot_t=jnp.exp(tot))


def _ssd_chunk_dir(act, tables, direction, h_ref, want_y):
    a, mask, rowp_t, f_t, etot_t = (tables[k] for k in ("a", "mask", "rowp_t", "f_t", "etot_t"))
    lane = lax.broadcasted_iota(jnp.int32, (CHUNK, PAIR), 1)
    low = lane < SSD_HEAD_DIM
    zero = jnp.zeros((CHUNK, PAIR), BF16)
    ys = []
    for g in range(SSD_GROUPS):
        bg = act[:, D_SSD + g * D_STATE:D_SSD + (g + 1) * D_STATE]
        bg_t = bg.astype(F32).T
        if want_y:
            cg = act[:, D_SSD + D_BC + g * D_STATE:D_SSD + D_BC + (g + 1) * D_STATE]
            cb = _dot_nt(cg, bg)
            y_off = _dot(cg, h_ref[g].astype(BF16))
        for j in range(PAIRS_PER_GROUP):
            col0 = (g * PAIRS_PER_GROUP + j) * PAIR
            xp = act[:, col0:col0 + PAIR]
            x_bd = jnp.concatenate([jnp.where(low, xp, zero), jnp.where(low, zero, xp)], axis=0)
            m_parts, s_parts, e_parts, etots = [], [], [], []
            for u in range(2):
                hc = direction * SSD_HEADS + g * SSD_HEADS_PER_GROUP + 2 * j + u
                s_parts.append((bg_t * f_t[hc:hc + 1, :]).astype(BF16))
                etots.append(jnp.broadcast_to(etot_t[hc:hc + 1, :], (D_STATE, PAIR)))
                if want_y:
                    col = jnp.broadcast_to(a[:, hc:hc + 1], (CHUNK, CHUNK))
                    decay = jnp.exp(jnp.where(mask, col - rowp_t[hc:hc + 1, :], NEG_BIG))
                    m_parts.append((cb * decay).astype(BF16))
                    e_parts.append(jnp.exp(col))
            hs = slice(j * PAIR, (j + 1) * PAIR)
            new_state = jnp.where(low, etots[0], etots[1]) * h_ref[g, :, hs] + _dot(
                jnp.concatenate(s_parts, axis=1), x_bd)
            if want_y:
                y_diag = _dot(jnp.concatenate(m_parts, axis=1), x_bd)
                ys.append(y_diag + y_off[:, hs] * jnp.where(low, e_parts[0], e_parts[1]))
            h_ref[g, :, hs] = new_state
    return jnp.concatenate(ys, axis=1) if want_y else None


def _ssd_ctx_kernel(act_ref, dt_ref, dtb_ref, a_ref, hout_ref, h_ref, *, n_chunks):
    dtb, a_neg = dtb_ref[...], a_ref[...]
    for direction in range(2):
        h_ref[...] = jnp.zeros(h_ref.shape, F32)
        order = range(n_chunks) if direction == 0 else range(n_chunks - 1, -1, -1)
        for c in order:
            rows = slice(c * CHUNK, (c + 1) * CHUNK)
            tables = _chunk_tables(dt_ref[0, rows, :], dtb, a_neg, direction)
            _ssd_chunk_dir(act_ref[0, rows, :], tables, direction, h_ref, False)
        hout_ref[0, direction] = h_ref[...]


def _ssd_ctx(act, dt, dtb, a_neg):
    b, s, _ = act.shape
    n_chunks = s // CHUNK
    const2 = lambda i: (0, 0)
    state = (SSD_GROUPS, D_STATE, D_GROUP)
    return pl.pallas_call(
        functools.partial(_ssd_ctx_kernel, n_chunks=n_chunks),
        grid=(b,),
        in_specs=[pl.BlockSpec((1, s, D_XBC), lambda i: (i, 0, 0)),
                  pl.BlockSpec((1, s, DT_PAD), lambda i: (i, 0, 0)),
                  pl.BlockSpec((1, DT_PAD), const2),
                  pl.BlockSpec((1, DT_PAD), const2)],
        out_specs=pl.BlockSpec((1, 2) + state, lambda i: (i, 0, 0, 0, 0)),
        out_shape=jax.ShapeDtypeStruct((b, 2) + state, F32),
        scratch_shapes=[pltpu.VMEM(state, F32)],
        compiler_params=_params("parallel"),
        name="ssd_ctx",
    )(act, dt, dtb, a_neg)


def _ssd_kernel(act_ref, dt_ref, z_ref, h0_ref, dtb_ref, a_ref, skip_ref, ng_ref, out_ref, y_ref, h_ref, *, n_chunks):
    j = pl.program_id(1)
    dtb, a_neg = dtb_ref[...], a_ref[...]

    @pl.when(j < n_chunks)
    def _forward():
        @pl.when(j == 0)
        def _():
            h_ref[...] = h0_ref[0, 0]

        act = act_ref[0]
        y = _ssd_chunk_dir(act, _chunk_tables(dt_ref[0], dtb, a_neg, 0), 0, h_ref, True)
        rows = pl.ds(pl.multiple_of(j * CHUNK, CHUNK), CHUNK)
        y_ref[rows, :] = y + act[:, 0:D_SSD].astype(F32) * skip_ref[...]

    @pl.when(j >= n_chunks)
    def _backward():
        @pl.when(j == n_chunks)
        def _():
            h_ref[...] = h0_ref[0, 1]

        c = 2 * n_chunks - 1 - j
        rows = pl.ds(pl.multiple_of(c * CHUNK, CHUNK), CHUNK)
        y = y_ref[rows, :] + _ssd_chunk_dir(act_ref[0], _chunk_tables(dt_ref[0], dtb, a_neg, 1), 1, h_ref, True)
        y = y * _silu(z_ref[0].astype(F32))
        parts = []
        for g in range(SSD_GROUPS):
            yg = y[:, g * D_GROUP:(g + 1) * D_GROUP]
            parts.append(yg * lax.rsqrt(jnp.mean(yg * yg, -1, keepdims=True) + 1e-6))
        out_ref[0] = (jnp.concatenate(parts, axis=1) * ng_ref[...]).astype(BF16)


def _ssd(act, dt, z, h0, dtb, a_neg, skip, ng):
    b, l, _ = act.shape
    nc = l // CHUNK

    def chunk_of(j):
        return jnp.where(j < nc, j, 2 * nc - 1 - j)

    def bwd_chunk(j):
        return jnp.where(j < nc, nc - 1, 2 * nc - 1 - j)

    const2 = lambda i, j: (0, 0)
    state = (SSD_GROUPS, D_STATE, D_GROUP)
    return pl.pallas_call(
        functools.partial(_ssd_kernel, n_chunks=nc),
        grid=(b, 2 * nc),
        in_specs=[pl.BlockSpec((1, CHUNK, D_XBC), lambda i, j: (i, chunk_of(j), 0)),
                  pl.BlockSpec((1, CHUNK, DT_PAD), lambda i, j: (i, chunk_of(j), 0)),
                  pl.BlockSpec((1, CHUNK, D_SSD), lambda i, j: (i, bwd_chunk(j), 0)),
                  pl.BlockSpec((1, 2) + state, lambda i, j: (i, 0, 0, 0, 0)),
                  pl.BlockSpec((1, DT_PAD), const2),
                  pl.BlockSpec((1, DT_PAD), const2),
                  pl.BlockSpec((1, D_SSD), const2),
                  pl.BlockSpec((1, D_SSD), const2)],
        out_specs=pl.BlockSpec((1, CHUNK, D_SSD), lambda i, j: (i, bwd_chunk(j), 0)),
        out_shape=jax.ShapeDtypeStruct((b, l, D_SSD), BF16),
        scratch_shapes=[pltpu.VMEM((l, D_SSD), F32),
                        pltpu.VMEM(state, F32)],
        compiler_params=_params("parallel", "arbitrary"),
        name="ssd",
    )(act, dt, z, h0, dtb, a_neg, skip, ng)


ATTN_ROWS = 128


def _attn_kernel(q_ref, kc_ref, kl_ref, vc_ref, vl_ref, o_ref, *, tq):
    def with_ones(v):
        lane = lax.broadcasted_iota(jnp.int32, (v.shape[0], LANES - HEAD_DIM), 1)
        return jnp.concatenate([v, jnp.where(lane == 0, 1.0, 0.0).astype(BF16)], axis=1)

    heads = []
    for kvh in range(ATTN_KV_HEADS):
        cols = slice(kvh * HEAD_DIM, (kvh + 1) * HEAD_DIM)
        heads.append((kc_ref[0, :, cols], kl_ref[0, :, cols], with_ones(vc_ref[0, :, cols]),
                      with_ones(vl_ref[0, :, cols])))
    s_ctx = kc_ref.shape[1]
    for r in range(tq // ATTN_ROWS):
        rows = slice(r * ATTN_ROWS, (r + 1) * ATTN_ROWS)
        for kvh in range(ATTN_KV_HEADS):
            kc, kl, vc, vl = heads[kvh]
            base = kvh * GQA_GROUP * HEAD_DIM
            q = jnp.concatenate(
                [q_ref[0, rows, base + g * HEAD_DIM:base + (g + 1) * HEAD_DIM] for g in range(GQA_GROUP)], axis=0)
            s = jnp.concatenate([_dot_nt(q, kc), _dot_nt(q, kl)], axis=1)
            p = jnp.exp2(s - jnp.max(s, -1, keepdims=True)).astype(BF16)
            pv = _dot(p[:, 0:s_ctx], vc) + _dot(p[:, s_ctx:], vl)
            o = pv[:, 0:HEAD_DIM] / pv[:, HEAD_DIM:HEAD_DIM + 1]
            o_ref[0, rows, base:base + GQA_GROUP * HEAD_DIM] = jnp.concatenate(
                [o[g * ATTN_ROWS:(g + 1) * ATTN_ROWS] for g in range(GQA_GROUP)], axis=1).astype(BF16)


def _attention(q, kc, kl, vc, vl, *, tq):
    b, l, _ = q.shape
    s_ctx = kc.shape[1]
    ctx_map = lambda i, t: (i, 0, 0)
    return pl.pallas_call(
        functools.partial(_attn_kernel, tq=tq),
        grid=(b, l // tq),
        in_specs=[pl.BlockSpec((1, tq, D_ATTN), lambda i, t: (i, t, 0)),
                  pl.BlockSpec((1, s_ctx, D_KV), ctx_map),
                  pl.BlockSpec((1, l, D_KV), ctx_map),
                  pl.BlockSpec((1, s_ctx, D_KV), ctx_map),
                  pl.BlockSpec((1, l, D_KV), ctx_map)],
        out_specs=pl.BlockSpec((1, tq, D_ATTN), lambda i, t: (i, t, 0)),
        out_shape=jax.ShapeDtypeStruct((b, l, D_ATTN), BF16),
        compiler_params=_params("parallel", "parallel"),
        name="attention",
    )(q, kc, kl, vc, vl)


def _first_index_of_max(vals, lane):
    m = jnp.max(vals, -1, keepdims=True)
    idx = jnp.min(jnp.where(vals == m, lane, LANES), -1, keepdims=True)
    return m, idx


def _router(logits):
    lane = lax.broadcasted_iota(jnp.int32, logits.shape, 1)
    is_e = lane < N_EXPERTS
    is_g = (lane >= N_EXPERTS) & (lane < 2 * N_EXPERTS)
    gl = jnp.where(is_g, logits, NEG_BIG)
    g_max, g_lane = _first_index_of_max(gl, lane)
    g_sum = jnp.sum(jnp.where(is_g, jnp.exp(gl - g_max), 0.0), -1, keepdims=True) * (1.0 / EXPERTS_PER_GROUP)
    g_val = 1.0 / g_sum
    g_idx = (g_lane - N_EXPERTS) // EXPERTS_PER_GROUP
    in_group = is_e & ((lane // EXPERTS_PER_GROUP) == g_idx)
    el = jnp.where(in_group, logits, NEG_BIG)
    m1, i1 = _first_index_of_max(el, lane)
    el2 = jnp.where(lane == i1, NEG_BIG, el)
    m2, i2 = _first_index_of_max(el2, lane)
    r = jnp.exp(m2 - m1)
    w1 = 1.0 / (1.0 + r)
    w2 = r * w1
    return g_val * (jnp.where(lane == i1, w1, 0.0) + jnp.where(lane == i2, w2, 0.0))


def _out_proj_kernel(y_ref, o_ref, x_ref, ga_ref, scf_ref, shf_ref, w_ref, g1_ref, b1_ref, wr_ref, br_ref,
                     x1_ref, t_ref, comb_ref, *, alpha):
    tm = x_ref.shape[1]
    n_sub = 2 if tm % (2 * SUBLANES * 2) == 0 else 1
    for r in range(n_sub):
        rows = slice(r * (tm // n_sub), (r + 1) * (tm // n_sub))
        mix = _dot(y_ref[0, rows, :], w_ref[0:D_SSD, :]) + _dot(o_ref[0, rows, :], w_ref[D_SSD:D_SSD + D_ATTN, :])
        x1 = _layer_norm(alpha * x_ref[0, rows, :] + ga_ref[0] * mix, g1_ref[...], b1_ref[...])
        x1_ref[0, rows, :] = x1
        t = x1 * (1.0 + scf_ref[0]) + shf_ref[0]
        t_hi, t_lo = _split2(t)
        t_ref[0, rows, :] = t_hi
        both = _dot(t_hi, wr_ref[...])
        logits = both[:, 0:LANES] + (both[:, LANES:2 * LANES] + _dot(t_lo, wr_ref[:, 0:LANES]))
        comb_ref[0, rows, :] = logits + br_ref[...]


def _out_proj(y, o, x, ga, scf, shf, w_out, g1, b1, wr, br, *, alpha, tm):
    b, l, d = x.shape
    row_map = lambda i, t: (i, t, 0)
    mod_map = lambda i, t: (i, 0, 0)
    const2 = lambda i, t: (0, 0)
    return pl.pallas_call(
        functools.partial(_out_proj_kernel, alpha=alpha),
        grid=(b, l // tm),
        in_specs=[pl.BlockSpec((1, tm, D_SSD), row_map),
                  pl.BlockSpec((1, tm, D_ATTN), row_map),
                  pl.BlockSpec((1, tm, d), row_map),
                  pl.BlockSpec((1, 1, d), mod_map),
                  pl.BlockSpec((1, 1, d), mod_map),
                  pl.BlockSpec((1, 1, d), mod_map),
                  pl.BlockSpec((D_SSD + D_ATTN, d), const2),
                  pl.BlockSpec((1, d), const2),
                  pl.BlockSpec((1, d), const2),
                  pl.BlockSpec((d, 2 * LANES), const2),
                  pl.BlockSpec((1, LANES), const2)],
        out_specs=[pl.BlockSpec((1, tm, d), row_map),
                   pl.BlockSpec((1, tm, d), row_map),
                   pl.BlockSpec((1, tm, LANES), row_map)],
        out_shape=[jax.ShapeDtypeStruct((b, l, d), F32),
                   jax.ShapeDtypeStruct((b, l, d), BF16),
                   jax.ShapeDtypeStruct((b, l, LANES), F32)],
        compiler_params=_params("parallel", "parallel"),
        name="out_proj",
    )(y, o, x, ga, scf, shf, w_out, g1, b1, wr, br)


def _moe_kernel(t_ref, comb_ref, x1_ref, gf_ref, wg_ref, wu_ref, wd_ref, g2_ref, b2_ref, out_ref, acc_ref, *, alpha):
    t = t_ref[0]
    comb = _router(comb_ref[0])
    lane = lax.broadcasted_iota(jnp.int32, comb.shape, 1)

    def group_out(g):
        hids = []
        for i in range(EXPERTS_PER_GROUP):
            e = g * EXPERTS_PER_GROUP + i
            ce = jnp.sum(jnp.where(lane == e, comb, 0.0), -1, keepdims=True)
            hids.append((_silu(_dot(t, wg_ref[e])) * _dot(t, wu_ref[e]) * ce).astype(BF16))
        return _dot(jnp.concatenate(hids, axis=1), wd_ref[g])

    acc_ref[...] = group_out(0)

    def body(g, carry):
        acc_ref[...] += group_out(g)
        return carry

    lax.fori_loop(1, MOE_GROUPS, body, 0)
    out_ref[0] = _layer_norm(alpha * x1_ref[0] + gf_ref[0] * acc_ref[...], g2_ref[...], b2_ref[...])


def _moe(t, comb, x1, gf, wg, wu, wd, g2, b2, *, alpha, tm):
    b, l, d = x1.shape
    row_map = lambda i, r: (i, r, 0)
    const2 = lambda i, r: (0, 0)
    const3 = lambda i, r: (0, 0, 0)
    resident = pl.Buffered(1)
    return pl.pallas_call(
        functools.partial(_moe_kernel, alpha=alpha),
        grid=(b, l // tm),
        in_specs=[pl.BlockSpec((1, tm, d), row_map),
                  pl.BlockSpec((1, tm, LANES), row_map),
                  pl.BlockSpec((1, tm, d), row_map),
                  pl.BlockSpec((1, 1, d), lambda i, r: (i, 0, 0)),
                  pl.BlockSpec(wg.shape, const3, pipeline_mode=resident),
                  pl.BlockSpec(wu.shape, const3, pipeline_mode=resident),
                  pl.BlockSpec(wd.shape, const3, pipeline_mode=resident),
                  pl.BlockSpec((1, d), const2),
                  pl.BlockSpec((1, d), const2)],
        out_specs=pl.BlockSpec((1, tm, d), row_map),
        out_shape=jax.ShapeDtypeStruct((b, l, d), F32),
        scratch_shapes=[pltpu.VMEM((tm, d), F32)],
        compiler_params=_params("parallel", "parallel"),
        name="moe",
    )(t, comb, x1, gf, wg, wu, wd, g2, b2)


def _rope_tables(n_rows):
    rows = jnp.broadcast_to(jnp.arange(n_rows, dtype=F32)[:, None], (n_rows, GRID_W)).reshape(-1)
    cols = jnp.broadcast_to(jnp.arange(GRID_W, dtype=F32)[None, :], (n_rows, GRID_W)).reshape(-1)
    axis_dim = HEAD_DIM // 2
    inv_freq = jnp.power(ROPE_THETA, -jnp.arange(0, axis_dim, 2, dtype=F32) / axis_dim)
    ang = jnp.concatenate([rows[:, None] * inv_freq, cols[:, None] * inv_freq], -1)
    cos, sin = jnp.cos(ang), jnp.sin(ang)
    cos_h = jnp.concatenate([cos, cos], -1)
    sin_h = jnp.concatenate([-sin, sin], -1)
    reps = LANES // HEAD_DIM
    return jnp.tile(cos_h, (1, reps)), jnp.tile(sin_h, (1, reps))


def _head_perm(n_heads):
    within = jnp.concatenate([jnp.arange(0, HEAD_DIM, 2), jnp.arange(1, HEAD_DIM, 2)])
    return (jnp.arange(n_heads)[:, None] * HEAD_DIM + within[None, :]).reshape(-1)


def kernel(x, c, ctx, c_ctx, w_mod, b_mod, w_in, conv_w, conv_b, dt_bias, a_log, d_skip, ssd_norm_g, q_norm_g,
           k_norm_g, w_out, ln1_g, ln1_b, w_rg, b_rg, w_re, b_re, w_gate, w_up, w_down, ln2_g, ln2_b):
    depth = w_in.shape[0]
    assert depth == 1, "single-layer block only"
    b, l, d = x.shape
    s_ctx = ctx.shape[1]
    assert d == D_MODEL and l % CHUNK == 0 and s_ctx % CHUNK == 0 and l % GRID_W == 0
    alpha = (2.0 * depth) ** 0.25

    m_rows = -(-(b + 1) // SUBLANES) * SUBLANES
    c_rows = jnp.zeros((m_rows, d), F32).at[:b].set(c).at[b].set(c_ctx)
    mod = _modulation(c_rows, w_mod[0], b_mod[0])
    sh_a, sc_a, g_a, sh_f, sc_f, g_f = [mod[:b, i * d:(i + 1) * d].reshape(b, 1, d) for i in range(6)]
    sh_c = mod[b:b + 1, 0:d].reshape(1, 1, d)
    sc_c = mod[b:b + 1, d:2 * d].reshape(1, 1, d)

    w = w_in[0]
    perm_q, perm_k = _head_perm(ATTN_Q_HEADS), _head_perm(ATTN_KV_HEADS)
    w_dt = jnp.pad(w[:, OFF_DT:OFF_K], ((0, 0), (0, DT_PAD - 2 * SSD_HEADS)))
    w_full = jnp.concatenate([w[:, :OFF_Q], w[:, OFF_Q:OFF_XBC][:, perm_q], w[:, OFF_XBC:OFF_DT],
                              w[:, OFF_K:OFF_V][:, perm_k], w[:, OFF_V:], w_dt], axis=1).astype(BF16)
    w_side = w_full[:, OFF_XBC:]
    within = perm_k[:HEAD_DIM]
    gq = jnp.tile(q_norm_g[0][within], ATTN_Q_HEADS).reshape(1, D_ATTN) * (HEAD_DIM ** -0.5 * math.log2(math.e))
    gk = jnp.tile(k_norm_g[0][within], ATTN_KV_HEADS).reshape(1, D_KV)
    lane = jnp.arange(MXU_DIM)
    ones_bd = (lane[:, None] // HEAD_DIM == lane[None, :] // HEAD_DIM).astype(BF16)
    cos, sin = _rope_tables(l // GRID_W)

    convw, convb = conv_w[0], conv_b[0].reshape(1, D_XBC)
    tm_in = 512 if l % 512 == 0 else CHUNK
    act_c, k_c, v_c, dt_c = _in_proj(ctx, sc_c, sh_c, w_side, ones_bd, gq, gk, convw, convb, None, None,
                                     with_zq=False, tm=s_ctx)
    z, q, act, k, v, dt = _in_proj(x, sc_a, sh_a, w_full, ones_bd, gq, gk, convw, convb, cos, sin,
                                   with_zq=True, tm=tm_in)

    pad_dt = lambda u: jnp.pad(u.reshape(1, 2 * SSD_HEADS).astype(F32), ((0, 0), (0, DT_PAD - 2 * SSD_HEADS)))
    dtb = pad_dt(dt_bias[0])
    a_neg = pad_dt(-jnp.exp(a_log[0].astype(F32)))
    skip = jnp.repeat(d_skip[0].astype(F32), SSD_HEAD_DIM).reshape(1, D_SSD)
    ng = ssd_norm_g[0].reshape(1, D_SSD)
    h_ctx = _ssd_ctx(act_c, dt_c, dtb, a_neg)
    y_ssd = _ssd(act, dt, z, h_ctx, dtb, a_neg, skip, ng)

    o = _attention(q, k_c, k, v_c, v, tq=512 if l % 512 == 0 else ATTN_ROWS)

    wr = jnp.concatenate([w_re[0], jnp.repeat(w_rg[0], EXPERTS_PER_GROUP, axis=1)], axis=1)
    wr = jnp.pad(wr, ((0, 0), (0, LANES - 2 * N_EXPERTS)))
    wr_hi = wr.astype(BF16)
    wr = jnp.concatenate([wr_hi, (wr - wr_hi.astype(F32)).astype(BF16)], axis=1)
    br = jnp.concatenate([b_re[0], jnp.repeat(b_rg[0], EXPERTS_PER_GROUP)])
    br = jnp.pad(br, (0, LANES - 2 * N_EXPERTS)).reshape(1, LANES)
    x1, t, comb = _out_proj(y_ssd, o, x, g_a, sc_f, sh_f, w_out[0].astype(BF16), ln1_g[0].reshape(1, d),
                            ln1_b[0].reshape(1, d), wr, br, alpha=alpha, tm=tm_in)

    wd = w_down[0].astype(BF16).reshape(MOE_GROUPS, EXPERTS_PER_GROUP * D_EXPERT, d)
    return _moe(t, comb, x1, g_f, w_gate[0].astype(BF16), w_up[0].astype(BF16), wd, ln2_g[0].reshape(1, d),
                ln2_b[0].reshape(1, d), alpha=alpha, tm=tm_in)
```

```python
import functools
import math

import jax
import jax.numpy as jnp
from jax import lax
from jax.experimental import pallas as pl
from jax.experimental.pallas import tpu as pltpu

F32 = jnp.float32
BF16 = jnp.bfloat16

D_MODEL = 1024
GRID_W = 64
SSD_HEADS = 16
SSD_HEAD_DIM = 64
D_SSD = SSD_HEADS * SSD_HEAD_DIM
SSD_GROUPS = 2
SSD_HEADS_PER_GROUP = SSD_HEADS // SSD_GROUPS
D_STATE = 128
CONV_W = 5
CHUNK = 128
D_BC = SSD_GROUPS * D_STATE
D_XBC = D_SSD + 2 * D_BC
ATTN_Q_HEADS = 16
ATTN_KV_HEADS = 4
HEAD_DIM = 64
GQA_GROUP = ATTN_Q_HEADS // ATTN_KV_HEADS
D_ATTN = ATTN_Q_HEADS * HEAD_DIM
D_KV = ATTN_KV_HEADS * HEAD_DIM
ROPE_THETA = 10000.0
MOE_GROUPS = 4
EXPERTS_PER_GROUP = 4
N_EXPERTS = MOE_GROUPS * EXPERTS_PER_GROUP
D_EXPERT = 256

OFF_Q = D_SSD
OFF_XBC = OFF_Q + D_ATTN
OFF_DT = OFF_XBC + D_XBC
OFF_K = OFF_DT + 2 * SSD_HEADS
OFF_V = OFF_K + D_KV

LANES = 128
SUBLANES = 8
MXU_DIM = 256
VMEM_LIMIT = 56 * 1024 * 1024

DT_PAD = LANES
HALO = SUBLANES
NEG_BIG = -1e30


def _dot(a, b):
    return jnp.dot(a, b, preferred_element_type=F32)


def _dot_nt(a, b):
    return lax.dot_general(a, b, (((1,), (1,)), ((), ())), preferred_element_type=F32)


def _split2(a):
    hi = a.astype(BF16)
    lo = (a - hi.astype(F32)).astype(BF16)
    return hi, lo


def _split3(a):
    hi = a.astype(BF16)
    r = a - hi.astype(F32)
    mid = r.astype(BF16)
    lo = (r - mid.astype(F32)).astype(BF16)
    return hi, mid, lo


def _dot_f32(a, b):
    a_hi, a_lo = _split2(a)
    b_hi, b_lo = _split2(b)
    return _dot(a_hi, b_hi) + (_dot(a_hi, b_lo) + _dot(a_lo, b_hi))


def _silu(x):
    return x / (1.0 + jnp.exp(-x))


def _softplus(x):
    return jnp.maximum(x, 0.0) + jnp.log(1.0 + jnp.exp(-jnp.abs(x)))


def _layer_norm(x, g, b):
    mu = jnp.mean(x, -1, keepdims=True)
    xc = x - mu
    var = jnp.mean(xc * xc, -1, keepdims=True)
    return xc * lax.rsqrt(var + 1e-5) * g + b


def _params(*sem):
    return pltpu.CompilerParams(dimension_semantics=sem, vmem_limit_bytes=VMEM_LIMIT)


def _mod_kernel(c_ref, w_ref, b_ref, o_ref):
    o_ref[...] = _dot_f32(_silu(c_ref[...]), w_ref[...]) + b_ref[...]


def _modulation(c_rows, w_mod, b_mod):
    m, d = c_rows.shape
    n = w_mod.shape[1]
    tn = 512
    return pl.pallas_call(
        _mod_kernel,
        grid=(n // tn,),
        in_specs=[pl.BlockSpec((m, d), lambda j: (0, 0)),
                  pl.BlockSpec((d, tn), lambda j: (0, j)),
                  pl.BlockSpec((1, tn), lambda j: (0, j))],
        out_specs=pl.BlockSpec((m, tn), lambda j: (0, j)),
        out_shape=jax.ShapeDtypeStruct((m, n), F32),
        compiler_params=_params("parallel"),
        name="modulation",
    )(c_rows, w_mod, b_mod.reshape(1, n))


def _head_sumsq(x2, ones_bd):
    hi, lo = _split2(x2)
    blocks = []
    for j in range(x2.shape[1] // MXU_DIM):
        sl = slice(j * MXU_DIM, (j + 1) * MXU_DIM)
        blocks.append(_dot(hi[:, sl], ones_bd) + _dot(lo[:, sl], ones_bd))
    return blocks[0] if len(blocks) == 1 else jnp.concatenate(blocks, axis=1)


def _rope_partner(x):
    n = x.shape[1]
    half = HEAD_DIM // 2
    lane = lax.broadcasted_iota(jnp.int32, x.shape, 1)
    first_half = (lane & (HEAD_DIM - 1)) < half
    return jnp.where(first_half, pltpu.roll(x, n - half, 1), pltpu.roll(x, half, 1))


def _norm_rope(x, gain, ones_bd, cos, sin):
    ss = _head_sumsq(x * x, ones_bd)
    xn = x * lax.rsqrt(ss * (1.0 / HEAD_DIM) + 1e-6) * gain
    if cos is not None:
        reps = x.shape[1] // cos.shape[1]
        c = jnp.concatenate([cos] * reps, axis=1) if reps > 1 else cos
        s = jnp.concatenate([sin] * reps, axis=1) if reps > 1 else sin
        xn = xn * c + _rope_partner(xn) * s
    return xn


def _in_proj_kernel(*refs, with_zq, with_rope, tm):
    it = iter(refs)
    x_ref, xprev_ref, xnext_ref, sc_ref, sh_ref, w_ref, ones_ref, gq_ref, gk_ref, convw_ref, convb_ref = (
        next(it) for _ in range(11))
    cos_ref = sin_ref = None
    if with_rope:
        cos_ref, sin_ref = next(it), next(it)
    if with_zq:
        z_ref, q_ref = next(it), next(it)
    xbc_ref, k_ref, v_ref, dt_ref, pad_ref = (next(it) for _ in range(5))

    t = pl.program_id(0)
    scale, shift = 1.0 + sc_ref[0], sh_ref[0]
    hx = (x_ref[0] * scale + shift).astype(BF16)
    ones_bd = ones_ref[...]
    cos = cos_ref[...] if with_rope else None
    sin = sin_ref[...] if with_rope else None
    off = 0
    if with_zq:
        z_ref[0] = _dot(hx, w_ref[:, 0:D_SSD]).astype(BF16)
        q = _dot(hx, w_ref[:, D_SSD:D_SSD + D_ATTN])
        q_ref[0] = _norm_rope(q, gq_ref[...], ones_bd, cos, sin).astype(BF16)
        off = D_SSD + D_ATTN
    x_ext = jnp.concatenate([xprev_ref[0], x_ref[0], xnext_ref[0]], axis=0)
    raw = _dot((x_ext * scale + shift).astype(BF16), w_ref[:, off:off + D_XBC])
    row = lax.broadcasted_iota(jnp.int32, (tm + 2 * HALO, 1), 0)
    inside = ((row >= HALO) | (t > 0)) & ((row < tm + HALO) | (t < pl.num_programs(0) - 1))
    pad_ref[...] = jnp.where(inside, raw, 0.0)
    acc = convb_ref[...]
    for j in range(CONV_W):
        start = HALO - CONV_W // 2 + j
        acc = acc + convw_ref[j:j + 1, :] * pad_ref[start:start + tm, :]
    xbc_ref[0] = _silu(acc).astype(BF16)
    off += D_XBC
    k = _dot(hx, w_ref[:, off:off + D_KV])
    k_ref[0] = _norm_rope(k, gk_ref[...], ones_bd, cos, sin).astype(BF16)
    off += D_KV
    v_ref[0] = _dot(hx, w_ref[:, off:off + D_KV]).astype(BF16)
    off += D_KV
    dt_ref[0] = _dot(hx, w_ref[:, off:off + DT_PAD])


def _in_proj(x, scale, shift, w, ones_bd, gq, gk, convw, convb, cos, sin, *, with_zq, tm):
    b, l, d = x.shape
    nw = w.shape[1]
    with_rope = cos is not None
    per_batch = scale.shape[0] == b and b > 1
    mod_map = (lambda t, i: (i, 0, 0)) if per_batch else (lambda t, i: (0, 0, 0))
    row_map = lambda t, i: (i, t, 0)
    const2 = lambda t, i: (0, 0)
    per = tm // HALO
    in_specs = [pl.BlockSpec((1, tm, d), row_map),
                pl.BlockSpec((1, HALO, d), lambda t, i: (i, jnp.maximum(t * per - 1, 0), 0)),
                pl.BlockSpec((1, HALO, d), lambda t, i: (i, jnp.minimum((t + 1) * per, l // HALO - 1), 0)),
                pl.BlockSpec((1, 1, d), mod_map),
                pl.BlockSpec((1, 1, d), mod_map),
                pl.BlockSpec((d, nw), const2),
                pl.BlockSpec((MXU_DIM, MXU_DIM), const2),
                pl.BlockSpec((1, D_ATTN), const2),
                pl.BlockSpec((1, D_KV), const2),
                pl.BlockSpec((CONV_W, D_XBC), const2),
                pl.BlockSpec((1, D_XBC), const2)]
    args = [x, x, x, scale, shift, w, ones_bd, gq, gk, convw, convb]
    if with_rope:
        in_specs += [pl.BlockSpec((tm, cos.shape[1]), lambda t, i: (t, 0))] * 2
        args += [cos, sin]
    out_shape, out_specs = [], []

    def add_out(width, dtype):
        out_shape.append(jax.ShapeDtypeStruct((b, l, width), dtype))
        out_specs.append(pl.BlockSpec((1, tm, width), row_map))

    if with_zq:
        add_out(D_SSD, BF16)
        add_out(D_ATTN, BF16)
    add_out(D_XBC, BF16)
    add_out(D_KV, BF16)
    add_out(D_KV, BF16)
    add_out(DT_PAD, F32)
    return pl.pallas_call(
        functools.partial(_in_proj_kernel, with_zq=with_zq, with_rope=with_rope, tm=tm),
        grid=(l // tm, b),
        in_specs=in_specs,
        out_specs=out_specs,
        out_shape=out_shape,
        scratch_shapes=[pltpu.VMEM((tm + 2 * HALO, D_XBC), F32)],
        compiler_params=_params("parallel", "parallel"),
        name="in_proj_latent" if with_zq else "in_proj_ctx",
    )(*args)


PAIR = 2 * SSD_HEAD_DIM
PAIRS_PER_GROUP = SSD_HEADS_PER_GROUP // 2
D_GROUP = SSD_HEADS_PER_GROUP * SSD_HEAD_DIM


def _chunk_tables(dt_raw, dtb, a_neg, direction):
    dtv = _softplus(dt_raw + dtb)
    qi = lax.broadcasted_iota(jnp.int32, (CHUNK, CHUNK), 0)
    si = lax.broadcasted_iota(jnp.int32, (CHUNK, CHUNK), 1)
    mask = (si <= qi) if direction == 0 else (si >= qi)
    tri = jnp.where(mask, 1.0, 0.0).astype(BF16)
    hi, mid, lo = _split3(dtv * a_neg)
    a = _dot(tri, hi) + (_dot(tri, mid) + _dot(tri, lo))
    a_t = a.T
    dtv_t = dtv.T
    last = CHUNK - 1 if direction == 0 else 0
    tot = a_t[:, last:last + 1]
    log2e = math.log2(math.e)
    return dict(a=a * log2e, mask=mask, rowp_t=a_t * log2e - jnp.log2(dtv_t), f_t=dtv_t * jnp.exp(tot - a_t),
                etot_t=jnp.exp(tot))


def _ssd_chunk_dir(act, tables, direction, h_ref, want_y):
    a, mask, rowp_t, f_t, etot_t = (tables[k] for k in ("a", "mask", "rowp_t", "f_t", "etot_t"))
    lane = lax.broadcasted_iota(jnp.int32, (CHUNK, PAIR), 1)
    low = lane < SSD_HEAD_DIM
    zero = jnp.zeros((CHUNK, PAIR), BF16)
    ys = []
    for g in range(SSD_GROUPS):
        bg = act[:, D_SSD + g * D_STATE:D_SSD + (g + 1) * D_STATE]
        bg_t = bg.astype(F32).T
        if want_y:
            cg = act[:, D_SSD + D_BC + g * D_STATE:D_SSD + D_BC + (g + 1) * D_STATE]
            cb = _dot_nt(cg, bg)
            y_off = _dot(cg, h_ref[g].astype(BF16))
        for j in range(PAIRS_PER_GROUP):
            col0 = (g * PAIRS_PER_GROUP + j) * PAIR
            xp = act[:, col0:col0 + PAIR]
            x_bd = jnp.concatenate([jnp.where(low, xp, zero), jnp.where(low, zero, xp)], axis=0)
            m_parts, s_parts, e_parts, etots = [], [], [], []
            for u in range(2):
                hc = direction * SSD_HEADS + g * SSD_HEADS_PER_GROUP + 2 * j + u
                s_parts.append((bg_t * f_t[hc:hc + 1, :]).astype(BF16))
                etots.append(jnp.broadcast_to(etot_t[hc:hc + 1, :], (D_STATE, PAIR)))
                if want_y:
                    col = jnp.broadcast_to(a[:, hc:hc + 1], (CHUNK, CHUNK))
                    decay = jnp.exp2(jnp.where(mask, col - rowp_t[hc:hc + 1, :], NEG_BIG))
                    m_parts.append((cb * decay).astype(BF16))
                    e_parts.append(jnp.exp2(col))
            hs = slice(j * PAIR, (j + 1) * PAIR)
            new_state = jnp.where(low, etots[0], etots[1]) * h_ref[g, :, hs] + _dot(
                jnp.concatenate(s_parts, axis=1), x_bd)
            if want_y:
                y_diag = _dot(jnp.concatenate(m_parts, axis=1), x_bd)
                ys.append(y_diag + y_off[:, hs] * jnp.where(low, e_parts[0], e_parts[1]))
            h_ref[g, :, hs] = new_state
    return jnp.concatenate(ys, axis=1) if want_y else None


def _ssd_ctx_kernel(act_ref, dt_ref, dtb_ref, a_ref, hout_ref, h_ref, *, n_chunks):
    dtb, a_neg = dtb_ref[...], a_ref[...]
    for direction in range(2):
        h_ref[...] = jnp.zeros(h_ref.shape, F32)
        order = range(n_chunks) if direction == 0 else range(n_chunks - 1, -1, -1)
        for c in order:
            rows = slice(c * CHUNK, (c + 1) * CHUNK)
            tables = _chunk_tables(dt_ref[0, rows, :], dtb, a_neg, direction)
            _ssd_chunk_dir(act_ref[0, rows, :], tables, direction, h_ref, False)
        hout_ref[0, direction] = h_ref[...]


def _ssd_ctx(act, dt, dtb, a_neg):
    b, s, _ = act.shape
    n_chunks = s // CHUNK
    const2 = lambda i: (0, 0)
    state = (SSD_GROUPS, D_STATE, D_GROUP)
    return pl.pallas_call(
        functools.partial(_ssd_ctx_kernel, n_chunks=n_chunks),
        grid=(b,),
        in_specs=[pl.BlockSpec((1, s, D_XBC), lambda i: (i, 0, 0)),
                  pl.BlockSpec((1, s, DT_PAD), lambda i: (i, 0, 0)),
                  pl.BlockSpec((1, DT_PAD), const2),
                  pl.BlockSpec((1, DT_PAD), const2)],
        out_specs=pl.BlockSpec((1, 2) + state, lambda i: (i, 0, 0, 0, 0)),
        out_shape=jax.ShapeDtypeStruct((b, 2) + state, F32),
        scratch_shapes=[pltpu.VMEM(state, F32)],
        compiler_params=_params("parallel"),
        name="ssd_ctx",
    )(act, dt, dtb, a_neg)


def _ssd_kernel(act_ref, dt_ref, z_ref, h0_ref, dtb_ref, a_ref, skip_ref, ng_ref, out_ref, y_ref, h_ref, *, n_chunks):
    j = pl.program_id(1)
    dtb, a_neg = dtb_ref[...], a_ref[...]

    @pl.when(j < n_chunks)
    def _forward():
        @pl.when(j == 0)
        def _():
            h_ref[...] = h0_ref[0, 0]

        act = act_ref[0]
        y = _ssd_chunk_dir(act, _chunk_tables(dt_ref[0], dtb, a_neg, 0), 0, h_ref, True)
        rows = pl.ds(pl.multiple_of(j * CHUNK, CHUNK), CHUNK)
        y_ref[rows, :] = y + act[:, 0:D_SSD].astype(F32) * skip_ref[...]

    @pl.when(j >= n_chunks)
    def _backward():
        @pl.when(j == n_chunks)
        def _():
            h_ref[...] = h0_ref[0, 1]

        c = 2 * n_chunks - 1 - j
        rows = pl.ds(pl.multiple_of(c * CHUNK, CHUNK), CHUNK)
        y = y_ref[rows, :] + _ssd_chunk_dir(act_ref[0], _chunk_tables(dt_ref[0], dtb, a_neg, 1), 1, h_ref, True)
        y = y * _silu(z_ref[0].astype(F32))
        parts = []
        for g in range(SSD_GROUPS):
            yg = y[:, g * D_GROUP:(g + 1) * D_GROUP]
            parts.append(yg * lax.rsqrt(jnp.mean(yg * yg, -1, keepdims=True) + 1e-6))
        out_ref[0] = (jnp.concatenate(parts, axis=1) * ng_ref[...]).astype(BF16)


def _ssd(act, dt, z, h0, dtb, a_neg, skip, ng):
    b, l, _ = act.shape
    nc = l // CHUNK

    def chunk_of(j):
        return jnp.where(j < nc, j, 2 * nc - 1 - j)

    def bwd_chunk(j):
        return jnp.where(j < nc, nc - 1, 2 * nc - 1 - j)

    const2 = lambda i, j: (0, 0)
    state = (SSD_GROUPS, D_STATE, D_GROUP)
    return pl.pallas_call(
        functools.partial(_ssd_kernel, n_chunks=nc),
        grid=(b, 2 * nc),
        in_specs=[pl.BlockSpec((1, CHUNK, D_XBC), lambda i, j: (i, chunk_of(j), 0)),
                  pl.BlockSpec((1, CHUNK, DT_PAD), lambda i, j: (i, chunk_of(j), 0)),
                  pl.BlockSpec((1, CHUNK, D_SSD), lambda i, j: (i, bwd_chunk(j), 0)),
                  pl.BlockSpec((1, 2) + state, lambda i, j: (i, 0, 0, 0, 0)),
                  pl.BlockSpec((1, DT_PAD), const2),
                  pl.BlockSpec((1, DT_PAD), const2),
                  pl.BlockSpec((1, D_SSD), const2),
                  pl.BlockSpec((1, D_SSD), const2)],
        out_specs=pl.BlockSpec((1, CHUNK, D_SSD), lambda i, j: (i, bwd_chunk(j), 0)),
        out_shape=jax.ShapeDtypeStruct((b, l, D_SSD), BF16),
        scratch_shapes=[pltpu.VMEM((l, D_SSD), F32),
                        pltpu.VMEM(state, F32)],
        compiler_params=_params("parallel", "arbitrary"),
        name="ssd",
    )(act, dt, z, h0, dtb, a_neg, skip, ng)


ATTN_ROWS = 128


def _attn_kernel(q_ref, kc_ref, kl_ref, vc_ref, vl_ref, o_ref, *, tq):
    def with_ones(v):
        lane = lax.broadcasted_iota(jnp.int32, (v.shape[0], LANES - HEAD_DIM), 1)
        return jnp.concatenate([v, jnp.where(lane == 0, 1.0, 0.0).astype(BF16)], axis=1)

    heads = []
    for kvh in range(ATTN_KV_HEADS):
        cols = slice(kvh * HEAD_DIM, (kvh + 1) * HEAD_DIM)
        heads.append((kc_ref[0, :, cols], kl_ref[0, :, cols], with_ones(vc_ref[0, :, cols]),
                      with_ones(vl_ref[0, :, cols])))
    s_ctx = kc_ref.shape[1]
    for r in range(tq // ATTN_ROWS):
        rows = slice(r * ATTN_ROWS, (r + 1) * ATTN_ROWS)
        for kvh in range(ATTN_KV_HEADS):
            kc, kl, vc, vl = heads[kvh]
            base = kvh * GQA_GROUP * HEAD_DIM
            q = jnp.concatenate(
                [q_ref[0, rows, base + g * HEAD_DIM:base + (g + 1) * HEAD_DIM] for g in range(GQA_GROUP)], axis=0)
            s = jnp.concatenate([_dot_nt(q, kc), _dot_nt(q, kl)], axis=1)
            p = jnp.exp2(s - jnp.max(s, -1, keepdims=True)).astype(BF16)
            pv = _dot(p[:, 0:s_ctx], vc) + _dot(p[:, s_ctx:], vl)
            o = pv[:, 0:HEAD_DIM] / pv[:, HEAD_DIM:HEAD_DIM + 1]
            o_ref[0, rows, base:base + GQA_GROUP * HEAD_DIM] = jnp.concatenate(
                [o[g * ATTN_ROWS:(g + 1) * ATTN_ROWS] for g in range(GQA_GROUP)], axis=1).astype(BF16)


def _attention(q, kc, kl, vc, vl, *, tq):
    b, l, _ = q.shape
    s_ctx = kc.shape[1]
    ctx_map = lambda i, t: (i, 0, 0)
    return pl.pallas_call(
        functools.partial(_attn_kernel, tq=tq),
        grid=(b, l // tq),
        in_specs=[pl.BlockSpec((1, tq, D_ATTN), lambda i, t: (i, t, 0)),
                  pl.BlockSpec((1, s_ctx, D_KV), ctx_map),
                  pl.BlockSpec((1, l, D_KV), ctx_map),
                  pl.BlockSpec((1, s_ctx, D_KV), ctx_map),
                  pl.BlockSpec((1, l, D_KV), ctx_map)],
        out_specs=pl.BlockSpec((1, tq, D_ATTN), lambda i, t: (i, t, 0)),
        out_shape=jax.ShapeDtypeStruct((b, l, D_ATTN), BF16),
        compiler_params=_params("parallel", "parallel"),
        name="attention",
    )(q, kc, kl, vc, vl)


def _first_index_of_max(vals, lane):
    m = jnp.max(vals, -1, keepdims=True)
    idx = jnp.min(jnp.where(vals == m, lane, LANES), -1, keepdims=True)
    return m, idx


def _router(logits):
    lane = lax.broadcasted_iota(jnp.int32, logits.shape, 1)
    is_e = lane < N_EXPERTS
    is_g = (lane >= N_EXPERTS) & (lane < 2 * N_EXPERTS)
    gl = jnp.where(is_g, logits, NEG_BIG)
    g_max, g_lane = _first_index_of_max(gl, lane)
    g_sum = jnp.sum(jnp.where(is_g, jnp.exp(gl - g_max), 0.0), -1, keepdims=True) * (1.0 / EXPERTS_PER_GROUP)
    g_val = 1.0 / g_sum
    g_idx = (g_lane - N_EXPERTS) // EXPERTS_PER_GROUP
    in_group = is_e & ((lane // EXPERTS_PER_GROUP) == g_idx)
    el = jnp.where(in_group, logits, NEG_BIG)
    m1, i1 = _first_index_of_max(el, lane)
    el2 = jnp.where(lane == i1, NEG_BIG, el)
    m2, i2 = _first_index_of_max(el2, lane)
    r = jnp.exp(m2 - m1)
    w1 = 1.0 / (1.0 + r)
    w2 = r * w1
    return g_val * (jnp.where(lane == i1, w1, 0.0) + jnp.where(lane == i2, w2, 0.0)), g_idx


def _out_proj_kernel(y_ref, o_ref, x_ref, ga_ref, scf_ref, shf_ref, w_ref, g1_ref, b1_ref, wr_ref, br_ref,
                     x1_ref, t_ref, comb_ref, *, alpha):
    tm = x_ref.shape[1]
    n_sub = 2 if tm % (2 * SUBLANES * 2) == 0 else 1
    for r in range(n_sub):
        rows = slice(r * (tm // n_sub), (r + 1) * (tm // n_sub))
        mix = _dot(y_ref[0, rows, :], w_ref[0:D_SSD, :]) + _dot(o_ref[0, rows, :], w_ref[D_SSD:D_SSD + D_ATTN, :])
        x1 = _layer_norm(alpha * x_ref[0, rows, :] + ga_ref[0] * mix, g1_ref[...], b1_ref[...])
        x1_ref[0, rows, :] = x1
        t = x1 * (1.0 + scf_ref[0]) + shf_ref[0]
        t_hi, t_lo = _split2(t)
        t_ref[0, rows, :] = t_hi
        both = _dot(t_hi, wr_ref[...])
        logits = both[:, 0:LANES] + (both[:, LANES:2 * LANES] + _dot(t_lo, wr_ref[:, 0:LANES]))
        comb_ref[0, rows, :] = logits + br_ref[...]


def _out_proj(y, o, x, ga, scf, shf, w_out, g1, b1, wr, br, *, alpha, tm):
    b, l, d = x.shape
    row_map = lambda i, t: (i, t, 0)
    mod_map = lambda i, t: (i, 0, 0)
    const2 = lambda i, t: (0, 0)
    return pl.pallas_call(
        functools.partial(_out_proj_kernel, alpha=alpha),
        grid=(b, l // tm),
        in_specs=[pl.BlockSpec((1, tm, D_SSD), row_map),
                  pl.BlockSpec((1, tm, D_ATTN), row_map),
                  pl.BlockSpec((1, tm, d), row_map),
                  pl.BlockSpec((1, 1, d), mod_map),
                  pl.BlockSpec((1, 1, d), mod_map),
                  pl.BlockSpec((1, 1, d), mod_map),
                  pl.BlockSpec((D_SSD + D_ATTN, d), const2),
                  pl.BlockSpec((1, d), const2),
                  pl.BlockSpec((1, d), const2),
                  pl.BlockSpec((d, 2 * LANES), const2),
                  pl.BlockSpec((1, LANES), const2)],
        out_specs=[pl.BlockSpec((1, tm, d), row_map),
                   pl.BlockSpec((1, tm, d), row_map),
                   pl.BlockSpec((1, tm, LANES), row_map)],
        out_shape=[jax.ShapeDtypeStruct((b, l, d), F32),
                   jax.ShapeDtypeStruct((b, l, d), BF16),
                   jax.ShapeDtypeStruct((b, l, LANES), F32)],
        compiler_params=_params("parallel", "parallel"),
        name="out_proj",
    )(y, o, x, ga, scf, shf, w_out, g1, b1, wr, br)


MOE_CAP = 160


def _moe_kernel(t_ref, comb_ref, x1_ref, gf_ref, wg_ref, wu_ref, wd_ref, g2_ref, b2_ref, out_ref, acc_ref,
                tc_ref, cw_ref, oc_ref, *, alpha):
    t = t_ref[0]
    tm = t.shape[0]
    comb, g_idx = _router(comb_ref[0])
    lane = lax.broadcasted_iota(jnp.int32, comb.shape, 1)

    def expert_block(rows_t, rows_w, g):
        hids = []
        for i in range(EXPERTS_PER_GROUP):
            e = g * EXPERTS_PER_GROUP + i
            ce = jnp.sum(jnp.where(lax.broadcasted_iota(jnp.int32, rows_w.shape, 1) == e, rows_w, 0.0), -1,
                         keepdims=True)
            hids.append((_silu(_dot(rows_t, wg_ref[e])) * _dot(rows_t, wu_ref[e]) * ce).astype(BF16))
        return _dot(jnp.concatenate(hids, axis=1), wd_ref[g])

    onehot = jnp.where(lane == g_idx, 1.0, 0.0)
    before = lax.broadcasted_iota(jnp.int32, (tm, tm), 1) <= lax.broadcasted_iota(jnp.int32, (tm, tm), 0)
    ranks = _dot(jnp.where(before, 1.0, 0.0).astype(BF16), onehot.astype(BF16))
    pos = jnp.sum(onehot * ranks, -1, keepdims=True) - 1.0
    fits = jnp.max(ranks) <= MOE_CAP
    slot = g_idx.astype(F32) * MOE_CAP + pos
    n_slots = MOE_GROUPS * MOE_CAP

    @pl.when(fits)
    def _grouped():
        slot_row = jnp.broadcast_to(slot, (tm, LANES)).T[0:1, :]
        to_slots = jnp.where(lax.broadcasted_iota(jnp.int32, (n_slots, tm), 0).astype(F32) == slot_row,
                             1.0, 0.0).astype(BF16)
        tc_ref[...] = _dot(to_slots, t).astype(BF16)
        c_hi, c_lo = _split2(comb)
        packed = jnp.where(lane < N_EXPERTS, c_hi, pltpu.roll(c_lo.astype(F32), N_EXPERTS, 1).astype(BF16))
        moved = _dot(to_slots, packed)
        cw_ref[...] = moved + pltpu.roll(moved, LANES - N_EXPERTS, 1)

        def body(g, carry):
            rows = pl.ds(pl.multiple_of(g * MOE_CAP, MOE_CAP), MOE_CAP)
            oc_ref[rows, :] = expert_block(tc_ref[rows, :], cw_ref[rows, :], g).astype(BF16)
            return carry

        lax.fori_loop(0, MOE_GROUPS, body, 0)
        from_slots = jnp.where(lax.broadcasted_iota(jnp.int32, (tm, n_slots), 1).astype(F32) == slot,
                               1.0, 0.0).astype(BF16)
        acc_ref[...] = _dot(from_slots, oc_ref[...])

    @pl.when(jnp.logical_not(fits))
    def _dense():
        acc_ref[...] = expert_block(t, comb, 0)

        def body(g, carry):
            acc_ref[...] += expert_block(t, comb, g)
            return carry

        lax.fori_loop(1, MOE_GROUPS, body, 0)

    out_ref[0] = _layer_norm(alpha * x1_ref[0] + gf_ref[0] * acc_ref[...], g2_ref[...], b2_ref[...])


def _moe(t, comb, x1, gf, wg, wu, wd, g2, b2, *, alpha, tm):
    b, l, d = x1.shape
    row_map = lambda i, r: (i, r, 0)
    const2 = lambda i, r: (0, 0)
    const3 = lambda i, r: (0, 0, 0)
    resident = pl.Buffered(1)
    return pl.pallas_call(
        functools.partial(_moe_kernel, alpha=alpha),
        grid=(b, l // tm),
        in_specs=[pl.BlockSpec((1, tm, d), row_map),
                  pl.BlockSpec((1, tm, LANES), row_map),
                  pl.BlockSpec((1, tm, d), row_map),
                  pl.BlockSpec((1, 1, d), lambda i, r: (i, 0, 0)),
                  pl.BlockSpec(wg.shape, const3, pipeline_mode=resident),
                  pl.BlockSpec(wu.shape, const3, pipeline_mode=resident),
                  pl.BlockSpec(wd.shape, const3, pipeline_mode=resident),
                  pl.BlockSpec((1, d), const2),
                  pl.BlockSpec((1, d), const2)],
        out_specs=pl.BlockSpec((1, tm, d), row_map),
        out_shape=jax.ShapeDtypeStruct((b, l, d), F32),
        scratch_shapes=[pltpu.VMEM((tm, d), F32),
                        pltpu.VMEM((MOE_GROUPS * MOE_CAP, d), BF16),
                        pltpu.VMEM((MOE_GROUPS * MOE_CAP, LANES), F32),
                        pltpu.VMEM((MOE_GROUPS * MOE_CAP, d), BF16)],
        compiler_params=_params("parallel", "parallel"),
        name="moe",
    )(t, comb, x1, gf, wg, wu, wd, g2, b2)


def _rope_tables(n_rows):
    rows = jnp.broadcast_to(jnp.arange(n_rows, dtype=F32)[:, None], (n_rows, GRID_W)).reshape(-1)
    cols = jnp.broadcast_to(jnp.arange(GRID_W, dtype=F32)[None, :], (n_rows, GRID_W)).reshape(-1)
    axis_dim = HEAD_DIM // 2
    inv_freq = jnp.power(ROPE_THETA, -jnp.arange(0, axis_dim, 2, dtype=F32) / axis_dim)
    ang = jnp.concatenate([rows[:, None] * inv_freq, cols[:, None] * inv_freq], -1)
    cos, sin = jnp.cos(ang), jnp.sin(ang)
    cos_h = jnp.concatenate([cos, cos], -1)
    sin_h = jnp.concatenate([-sin, sin], -1)
    reps = LANES // HEAD_DIM
    return jnp.tile(cos_h, (1, reps)), jnp.tile(sin_h, (1, reps))


def _head_perm(n_heads):
    within = jnp.concatenate([jnp.arange(0, HEAD_DIM, 2), jnp.arange(1, HEAD_DIM, 2)])
    return (jnp.arange(n_heads)[:, None] * HEAD_DIM + within[None, :]).reshape(-1)


def kernel(x, c, ctx, c_ctx, w_mod, b_mod, w_in, conv_w, conv_b, dt_bias, a_log, d_skip, ssd_norm_g, q_norm_g,
           k_norm_g, w_out, ln1_g, ln1_b, w_rg, b_rg, w_re, b_re, w_gate, w_up, w_down, ln2_g, ln2_b):
    depth = w_in.shape[0]
    assert depth == 1, "single-layer block only"
    b, l, d = x.shape
    s_ctx = ctx.shape[1]
    assert d == D_MODEL and l % CHUNK == 0 and s_ctx % CHUNK == 0 and l % GRID_W == 0
    alpha = (2.0 * depth) ** 0.25

    m_rows = -(-(b + 1) // SUBLANES) * SUBLANES
    c_rows = jnp.zeros((m_rows, d), F32).at[:b].set(c).at[b].set(c_ctx)
    mod = _modulation(c_rows, w_mod[0], b_mod[0])
    sh_a, sc_a, g_a, sh_f, sc_f, g_f = [mod[:b, i * d:(i + 1) * d].reshape(b, 1, d) for i in range(6)]
    sh_c = mod[b:b + 1, 0:d].reshape(1, 1, d)
    sc_c = mod[b:b + 1, d:2 * d].reshape(1, 1, d)

    w = w_in[0]
    perm_q, perm_k = _head_perm(ATTN_Q_HEADS), _head_perm(ATTN_KV_HEADS)
    w_dt = jnp.pad(w[:, OFF_DT:OFF_K], ((0, 0), (0, DT_PAD - 2 * SSD_HEADS)))
    w_full = jnp.concatenate([w[:, :OFF_Q], w[:, OFF_Q:OFF_XBC][:, perm_q], w[:, OFF_XBC:OFF_DT],
                              w[:, OFF_K:OFF_V][:, perm_k], w[:, OFF_V:], w_dt], axis=1).astype(BF16)
    w_side = w_full[:, OFF_XBC:]
    within = perm_k[:HEAD_DIM]
    gq = jnp.tile(q_norm_g[0][within], ATTN_Q_HEADS).reshape(1, D_ATTN) * (HEAD_DIM ** -0.5 * math.log2(math.e))
    gk = jnp.tile(k_norm_g[0][within], ATTN_KV_HEADS).reshape(1, D_KV)
    lane = jnp.arange(MXU_DIM)
    ones_bd = (lane[:, None] // HEAD_DIM == lane[None, :] // HEAD_DIM).astype(BF16)
    cos, sin = _rope_tables(l // GRID_W)

    convw, convb = conv_w[0], conv_b[0].reshape(1, D_XBC)
    tm_in = 512 if l % 512 == 0 else CHUNK
    act_c, k_c, v_c, dt_c = _in_proj(ctx, sc_c, sh_c, w_side, ones_bd, gq, gk, convw, convb, None, None,
                                     with_zq=False, tm=s_ctx)
    z, q, act, k, v, dt = _in_proj(x, sc_a, sh_a, w_full, ones_bd, gq, gk, convw, convb, cos, sin,
                                   with_zq=True, tm=tm_in)

    pad_dt = lambda u: jnp.pad(u.reshape(1, 2 * SSD_HEADS).astype(F32), ((0, 0), (0, DT_PAD - 2 * SSD_HEADS)))
    dtb = pad_dt(dt_bias[0])
    a_neg = pad_dt(-jnp.exp(a_log[0].astype(F32)))
    skip = jnp.repeat(d_skip[0].astype(F32), SSD_HEAD_DIM).reshape(1, D_SSD)
    ng = ssd_norm_g[0].reshape(1, D_SSD)
    h_ctx = _ssd_ctx(act_c, dt_c, dtb, a_neg)
    y_ssd = _ssd(act, dt, z, h_ctx, dtb, a_neg, skip, ng)

    o = _attention(q, k_c, k, v_c, v, tq=512 if l % 512 == 0 else ATTN_ROWS)

    wr = jnp.concatenate([w_re[0], jnp.repeat(w_rg[0], EXPERTS_PER_GROUP, axis=1)], axis=1)
    wr = jnp.pad(wr, ((0, 0), (0, LANES - 2 * N_EXPERTS)))
    wr_hi = wr.astype(BF16)
    wr = jnp.concatenate([wr_hi, (wr - wr_hi.astype(F32)).astype(BF16)], axis=1)
    br = jnp.concatenate([b_re[0], jnp.repeat(b_rg[0], EXPERTS_PER_GROUP)])
    br = jnp.pad(br, (0, LANES - 2 * N_EXPERTS)).reshape(1, LANES)
    x1, t, comb = _out_proj(y_ssd, o, x, g_a, sc_f, sh_f, w_out[0].astype(BF16), ln1_g[0].reshape(1, d),
                            ln1_b[0].reshape(1, d), wr, br, alpha=alpha, tm=tm_in)

    wd = w_down[0].astype(BF16).reshape(MOE_GROUPS, EXPERTS_PER_GROUP * D_EXPERT, d)
    return _moe(t, comb, x1, g_f, w_gate[0].astype(BF16), w_up[0].astype(BF16), wd, ln2_g[0].reshape(1, d),
                ln2_b[0].reshape(1, d), alpha=alpha, tm=tm_in)
```

```python
import functools
import math

import jax
import jax.numpy as jnp
from jax import lax
from jax.experimental import pallas as pl
from jax.experimental.pallas import tpu as pltpu

F32 = jnp.float32
BF16 = jnp.bfloat16

D_MODEL = 1024
GRID_W = 64
SSD_HEADS = 16
SSD_HEAD_DIM = 64
D_SSD = SSD_HEADS * SSD_HEAD_DIM
SSD_GROUPS = 2
SSD_HEADS_PER_GROUP = SSD_HEADS // SSD_GROUPS
D_STATE = 128
CONV_W = 5
CHUNK = 128
D_BC = SSD_GROUPS * D_STATE
D_XBC = D_SSD + 2 * D_BC
ATTN_Q_HEADS = 16
ATTN_KV_HEADS = 4
HEAD_DIM = 64
GQA_GROUP = ATTN_Q_HEADS // ATTN_KV_HEADS
D_ATTN = ATTN_Q_HEADS * HEAD_DIM
D_KV = ATTN_KV_HEADS * HEAD_DIM
ROPE_THETA = 10000.0
MOE_GROUPS = 4
EXPERTS_PER_GROUP = 4
N_EXPERTS = MOE_GROUPS * EXPERTS_PER_GROUP
D_EXPERT = 256

OFF_Q = D_SSD
OFF_XBC = OFF_Q + D_ATTN
OFF_DT = OFF_XBC + D_XBC
OFF_K = OFF_DT + 2 * SSD_HEADS
OFF_V = OFF_K + D_KV

LANES = 128
SUBLANES = 8
MXU_DIM = 256
VMEM_LIMIT = 56 * 1024 * 1024

DT_PAD = LANES
HALO = SUBLANES
NEG_BIG = -1e30


def _dot(a, b):
    return jnp.dot(a, b, preferred_element_type=F32)


def _dot_nt(a, b):
    return lax.dot_general(a, b, (((1,), (1,)), ((), ())), preferred_element_type=F32)


def _split2(a):
    hi = a.astype(BF16)
    lo = (a - hi.astype(F32)).astype(BF16)
    return hi, lo


def _split3(a):
    hi = a.astype(BF16)
    r = a - hi.astype(F32)
    mid = r.astype(BF16)
    lo = (r - mid.astype(F32)).astype(BF16)
    return hi, mid, lo


def _dot_f32(a, b):
    a_hi, a_lo = _split2(a)
    b_hi, b_lo = _split2(b)
    return _dot(a_hi, b_hi) + (_dot(a_hi, b_lo) + _dot(a_lo, b_hi))


def _silu(x):
    return x / (1.0 + jnp.exp(-x))


def _softplus(x):
    return jnp.maximum(x, 0.0) + jnp.log(1.0 + jnp.exp(-jnp.abs(x)))


def _layer_norm(x, g, b):
    mu = jnp.mean(x, -1, keepdims=True)
    xc = x - mu
    var = jnp.mean(xc * xc, -1, keepdims=True)
    return xc * lax.rsqrt(var + 1e-5) * g + b


def _params(*sem):
    return pltpu.CompilerParams(dimension_semantics=sem, vmem_limit_bytes=VMEM_LIMIT)


def _mod_kernel(c_ref, w_ref, b_ref, o_ref):
    o_ref[...] = _dot_f32(_silu(c_ref[...]), w_ref[...]) + b_ref[...]


def _modulation(c_rows, w_mod, b_mod):
    m, d = c_rows.shape
    n = w_mod.shape[1]
    tn = 512
    return pl.pallas_call(
        _mod_kernel,
        grid=(n // tn,),
        in_specs=[pl.BlockSpec((m, d), lambda j: (0, 0)),
                  pl.BlockSpec((d, tn), lambda j: (0, j)),
                  pl.BlockSpec((1, tn), lambda j: (0, j))],
        out_specs=pl.BlockSpec((m, tn), lambda j: (0, j)),
        out_shape=jax.ShapeDtypeStruct((m, n), F32),
        compiler_params=_params("parallel"),
        name="modulation",
    )(c_rows, w_mod, b_mod.reshape(1, n))


def _head_sumsq(x2, ones_bd):
    hi, lo = _split2(x2)
    blocks = []
    for j in range(x2.shape[1] // MXU_DIM):
        sl = slice(j * MXU_DIM, (j + 1) * MXU_DIM)
        blocks.append(_dot(hi[:, sl], ones_bd) + _dot(lo[:, sl], ones_bd))
    return blocks[0] if len(blocks) == 1 else jnp.concatenate(blocks, axis=1)


def _rope_partner(x):
    n = x.shape[1]
    half = HEAD_DIM // 2
    lane = lax.broadcasted_iota(jnp.int32, x.shape, 1)
    first_half = (lane & (HEAD_DIM - 1)) < half
    return jnp.where(first_half, pltpu.roll(x, n - half, 1), pltpu.roll(x, half, 1))


def _norm_rope(x, gain, ones_bd, cos, sin):
    ss = _head_sumsq(x * x, ones_bd)
    xn = x * lax.rsqrt(ss * (1.0 / HEAD_DIM) + 1e-6) * gain
    if cos is not None:
        reps = x.shape[1] // cos.shape[1]
        c = jnp.concatenate([cos] * reps, axis=1) if reps > 1 else cos
        s = jnp.concatenate([sin] * reps, axis=1) if reps > 1 else sin
        xn = xn * c + _rope_partner(xn) * s
    return xn


def _in_proj_kernel(*refs, with_zq, with_rope, tm):
    it = iter(refs)
    x_ref, xprev_ref, xnext_ref, sc_ref, sh_ref, w_ref, ones_ref, gq_ref, gk_ref, convw_ref, convb_ref = (
        next(it) for _ in range(11))
    cos_ref = sin_ref = None
    if with_rope:
        cos_ref, sin_ref = next(it), next(it)
    if with_zq:
        z_ref, q_ref = next(it), next(it)
    xbc_ref, k_ref, v_ref, dt_ref, pad_ref = (next(it) for _ in range(5))

    t = pl.program_id(0)
    scale, shift = 1.0 + sc_ref[0], sh_ref[0]
    hx = (x_ref[0] * scale + shift).astype(BF16)
    ones_bd = ones_ref[...]
    cos = cos_ref[...] if with_rope else None
    sin = sin_ref[...] if with_rope else None
    off = 0
    if with_zq:
        z_ref[0] = _dot(hx, w_ref[:, 0:D_SSD]).astype(BF16)
        q = _dot(hx, w_ref[:, D_SSD:D_SSD + D_ATTN])
        q_ref[0] = _norm_rope(q, gq_ref[...], ones_bd, cos, sin).astype(BF16)
        off = D_SSD + D_ATTN
    x_ext = jnp.concatenate([xprev_ref[0], x_ref[0], xnext_ref[0]], axis=0)
    raw = _dot((x_ext * scale + shift).astype(BF16), w_ref[:, off:off + D_XBC])
    row = lax.broadcasted_iota(jnp.int32, (tm + 2 * HALO, 1), 0)
    inside = ((row >= HALO) | (t > 0)) & ((row < tm + HALO) | (t < pl.num_programs(0) - 1))
    pad_ref[...] = jnp.where(inside, raw, 0.0)
    acc = convb_ref[...]
    for j in range(CONV_W):
        start = HALO - CONV_W // 2 + j
        acc = acc + convw_ref[j:j + 1, :] * pad_ref[start:start + tm, :]
    xbc_ref[0] = _silu(acc).astype(BF16)
    off += D_XBC
    k = _dot(hx, w_ref[:, off:off + D_KV])
    k_ref[0] = _norm_rope(k, gk_ref[...], ones_bd, cos, sin).astype(BF16)
    off += D_KV
    v_ref[0] = _dot(hx, w_ref[:, off:off + D_KV]).astype(BF16)
    off += D_KV
    dt_ref[0] = _dot(hx, w_ref[:, off:off + DT_PAD])


def _in_proj(x, scale, shift, w, ones_bd, gq, gk, convw, convb, cos, sin, *, with_zq, tm):
    b, l, d = x.shape
    nw = w.shape[1]
    with_rope = cos is not None
    per_batch = scale.shape[0] == b and b > 1
    mod_map = (lambda t, i: (i, 0, 0)) if per_batch else (lambda t, i: (0, 0, 0))
    row_map = lambda t, i: (i, t, 0)
    const2 = lambda t, i: (0, 0)
    per = tm // HALO
    in_specs = [pl.BlockSpec((1, tm, d), row_map),
                pl.BlockSpec((1, HALO, d), lambda t, i: (i, jnp.maximum(t * per - 1, 0), 0)),
                pl.BlockSpec((1, HALO, d), lambda t, i: (i, jnp.minimum((t + 1) * per, l // HALO - 1), 0)),
                pl.BlockSpec((1, 1, d), mod_map),
                pl.BlockSpec((1, 1, d), mod_map),
                pl.BlockSpec((d, nw), const2),
                pl.BlockSpec((MXU_DIM, MXU_DIM), const2),
                pl.BlockSpec((1, D_ATTN), const2),
                pl.BlockSpec((1, D_KV), const2),
                pl.BlockSpec((CONV_W, D_XBC), const2),
                pl.BlockSpec((1, D_XBC), const2)]
    args = [x, x, x, scale, shift, w, ones_bd, gq, gk, convw, convb]
    if with_rope:
        in_specs += [pl.BlockSpec((tm, cos.shape[1]), lambda t, i: (t, 0))] * 2
        args += [cos, sin]
    out_shape, out_specs = [], []

    def add_out(width, dtype):
        out_shape.append(jax.ShapeDtypeStruct((b, l, width), dtype))
        out_specs.append(pl.BlockSpec((1, tm, width), row_map))

    if with_zq:
        add_out(D_SSD, BF16)
        add_out(D_ATTN, BF16)
    add_out(D_XBC, BF16)
    add_out(D_KV, BF16)
    add_out(D_KV, BF16)
    add_out(DT_PAD, F32)
    return pl.pallas_call(
        functools.partial(_in_proj_kernel, with_zq=with_zq, with_rope=with_rope, tm=tm),
        grid=(l // tm, b),
        in_specs=in_specs,
        out_specs=out_specs,
        out_shape=out_shape,
        scratch_shapes=[pltpu.VMEM((tm + 2 * HALO, D_XBC), F32)],
        compiler_params=_params("parallel", "parallel"),
        name="in_proj_latent" if with_zq else "in_proj_ctx",
    )(*args)


PAIR = 2 * SSD_HEAD_DIM
PAIRS_PER_GROUP = SSD_HEADS_PER_GROUP // 2
D_GROUP = SSD_HEADS_PER_GROUP * SSD_HEAD_DIM


def _chunk_tables(dt_raw, dtb, a_neg, direction):
    dtv = _softplus(dt_raw + dtb)
    qi = lax.broadcasted_iota(jnp.int32, (CHUNK, CHUNK), 0)
    si = lax.broadcasted_iota(jnp.int32, (CHUNK, CHUNK), 1)
    mask = (si <= qi) if direction == 0 else (si >= qi)
    tri = jnp.where(mask, 1.0, 0.0).astype(BF16)
    hi, mid, lo = _split3(dtv * a_neg)
    a = _dot(tri, hi) + (_dot(tri, mid) + _dot(tri, lo))
    a_t = a.T
    dtv_t = dtv.T
    last = CHUNK - 1 if direction == 0 else 0
    tot = a_t[:, last:last + 1]
    log2e = math.log2(math.e)
    return dict(a=a * log2e, mask=mask, rowp_t=a_t * log2e - jnp.log2(dtv_t), f_t=dtv_t * jnp.exp(tot - a_t),
                etot_t=jnp.exp(tot))


def _ssd_chunk_dir(act, tables, direction, h_ref, want_y):
    a, mask, rowp_t, f_t, etot_t = (tables[k] for k in ("a", "mask", "rowp_t", "f_t", "etot_t"))
    lane = lax.broadcasted_iota(jnp.int32, (CHUNK, PAIR), 1)
    low = lane < SSD_HEAD_DIM
    zero = jnp.zeros((CHUNK, PAIR), BF16)
    ys = []
    for g in range(SSD_GROUPS):
        bg = act[:, D_SSD + g * D_STATE:D_SSD + (g + 1) * D_STATE]
        bg_t = bg.astype(F32).T
        if want_y:
            cg = act[:, D_SSD + D_BC + g * D_STATE:D_SSD + D_BC + (g + 1) * D_STATE]
            cb = _dot_nt(cg, bg)
            y_off = _dot(cg, h_ref[g].astype(BF16))
        for j in range(PAIRS_PER_GROUP):
            col0 = (g * PAIRS_PER_GROUP + j) * PAIR
            xp = act[:, col0:col0 + PAIR]
            x_bd = jnp.concatenate([jnp.where(low, xp, zero), jnp.where(low, zero, xp)], axis=0)
            m_parts, s_parts, e_parts, etots = [], [], [], []
            for u in range(2):
                hc = direction * SSD_HEADS + g * SSD_HEADS_PER_GROUP + 2 * j + u
                s_parts.append((bg_t * f_t[hc:hc + 1, :]).astype(BF16))
                etots.append(jnp.broadcast_to(etot_t[hc:hc + 1, :], (D_STATE, PAIR)))
                if want_y:
                    col = jnp.broadcast_to(a[:, hc:hc + 1], (CHUNK, CHUNK))
                    decay = jnp.exp2(jnp.where(mask, col - rowp_t[hc:hc + 1, :], NEG_BIG))
                    m_parts.append((cb * decay).astype(BF16))
                    e_parts.append(jnp.exp2(col))
            hs = slice(j * PAIR, (j + 1) * PAIR)
            new_state = jnp.where(low, etots[0], etots[1]) * h_ref[g, :, hs] + _dot(
                jnp.concatenate(s_parts, axis=1), x_bd)
            if want_y:
                y_diag = _dot(jnp.concatenate(m_parts, axis=1), x_bd)
                ys.append(y_diag + y_off[:, hs] * jnp.where(low, e_parts[0], e_parts[1]))
            h_ref[g, :, hs] = new_state
    return jnp.concatenate(ys, axis=1) if want_y else None


def _ssd_ctx_kernel(act_ref, dt_ref, dtb_ref, a_ref, hout_ref, h_ref, *, n_chunks):
    dtb, a_neg = dtb_ref[...], a_ref[...]
    for direction in range(2):
        h_ref[...] = jnp.zeros(h_ref.shape, F32)
        order = range(n_chunks) if direction == 0 else range(n_chunks - 1, -1, -1)
        for c in order:
            rows = slice(c * CHUNK, (c + 1) * CHUNK)
            tables = _chunk_tables(dt_ref[0, rows, :], dtb, a_neg, direction)
            _ssd_chunk_dir(act_ref[0, rows, :], tables, direction, h_ref, False)
        hout_ref[0, direction] = h_ref[...]


def _ssd_ctx(act, dt, dtb, a_neg):
    b, s, _ = act.shape
    n_chunks = s // CHUNK
    const2 = lambda i: (0, 0)
    state = (SSD_GROUPS, D_STATE, D_GROUP)
    return pl.pallas_call(
        functools.partial(_ssd_ctx_kernel, n_chunks=n_chunks),
        grid=(b,),
        in_specs=[pl.BlockSpec((1, s, D_XBC), lambda i: (i, 0, 0)),
                  pl.BlockSpec((1, s, DT_PAD), lambda i: (i, 0, 0)),
                  pl.BlockSpec((1, DT_PAD), const2),
                  pl.BlockSpec((1, DT_PAD), const2)],
        out_specs=pl.BlockSpec((1, 2) + state, lambda i: (i, 0, 0, 0, 0)),
        out_shape=jax.ShapeDtypeStruct((b, 2) + state, F32),
        scratch_shapes=[pltpu.VMEM(state, F32)],
        compiler_params=_params("parallel"),
        name="ssd_ctx",
    )(act, dt, dtb, a_neg)


def _ssd_kernel(act_ref, dt_ref, z_ref, h0_ref, dtb_ref, a_ref, skip_ref, ng_ref, out_ref, y_ref, h_ref, *, n_chunks):
    j = pl.program_id(1)
    dtb, a_neg = dtb_ref[...], a_ref[...]

    @pl.when(j < n_chunks)
    def _forward():
        @pl.when(j == 0)
        def _():
            h_ref[...] = h0_ref[0, 0]

        act = act_ref[0]
        y = _ssd_chunk_dir(act, _chunk_tables(dt_ref[0], dtb, a_neg, 0), 0, h_ref, True)
        rows = pl.ds(pl.multiple_of(j * CHUNK, CHUNK), CHUNK)
        y_ref[rows, :] = y + act[:, 0:D_SSD].astype(F32) * skip_ref[...]

    @pl.when(j >= n_chunks)
    def _backward():
        @pl.when(j == n_chunks)
        def _():
            h_ref[...] = h0_ref[0, 1]

        c = 2 * n_chunks - 1 - j
        rows = pl.ds(pl.multiple_of(c * CHUNK, CHUNK), CHUNK)
        y = y_ref[rows, :] + _ssd_chunk_dir(act_ref[0], _chunk_tables(dt_ref[0], dtb, a_neg, 1), 1, h_ref, True)
        y = y * _silu(z_ref[0].astype(F32))
        parts = []
        for g in range(SSD_GROUPS):
            yg = y[:, g * D_GROUP:(g + 1) * D_GROUP]
            parts.append(yg * lax.rsqrt(jnp.mean(yg * yg, -1, keepdims=True) + 1e-6))
        out_ref[0] = (jnp.concatenate(parts, axis=1) * ng_ref[...]).astype(BF16)


def _ssd(act, dt, z, h0, dtb, a_neg, skip, ng):
    b, l, _ = act.shape
    nc = l // CHUNK

    def chunk_of(j):
        return jnp.where(j < nc, j, 2 * nc - 1 - j)

    def bwd_chunk(j):
        return jnp.where(j < nc, nc - 1, 2 * nc - 1 - j)

    const2 = lambda i, j: (0, 0)
    state = (SSD_GROUPS, D_STATE, D_GROUP)
    return pl.pallas_call(
        functools.partial(_ssd_kernel, n_chunks=nc),
        grid=(b, 2 * nc),
        in_specs=[pl.BlockSpec((1, CHUNK, D_XBC), lambda i, j: (i, chunk_of(j), 0)),
                  pl.BlockSpec((1, CHUNK, DT_PAD), lambda i, j: (i, chunk_of(j), 0)),
                  pl.BlockSpec((1, CHUNK, D_SSD), lambda i, j: (i, bwd_chunk(j), 0)),
                  pl.BlockSpec((1, 2) + state, lambda i, j: (i, 0, 0, 0, 0)),
                  pl.BlockSpec((1, DT_PAD), const2),
                  pl.BlockSpec((1, DT_PAD), const2),
                  pl.BlockSpec((1, D_SSD), const2),
                  pl.BlockSpec((1, D_SSD), const2)],
        out_specs=pl.BlockSpec((1, CHUNK, D_SSD), lambda i, j: (i, bwd_chunk(j), 0)),
        out_shape=jax.ShapeDtypeStruct((b, l, D_SSD), BF16),
        scratch_shapes=[pltpu.VMEM((l, D_SSD), F32),
                        pltpu.VMEM(state, F32)],
        compiler_params=_params("parallel", "arbitrary"),
        name="ssd",
    )(act, dt, z, h0, dtb, a_neg, skip, ng)


ATTN_ROWS = 128


def _attn_kernel(q_ref, kc_ref, kl_ref, vc_ref, vl_ref, o_ref, *, tq):
    def with_ones(v):
        lane = lax.broadcasted_iota(jnp.int32, (v.shape[0], LANES - HEAD_DIM), 1)
        return jnp.concatenate([v, jnp.where(lane == 0, 1.0, 0.0).astype(BF16)], axis=1)

    heads = []
    for kvh in range(ATTN_KV_HEADS):
        cols = slice(kvh * HEAD_DIM, (kvh + 1) * HEAD_DIM)
        heads.append((kc_ref[0, :, cols], kl_ref[0, :, cols], with_ones(vc_ref[0, :, cols]),
                      with_ones(vl_ref[0, :, cols])))
    s_ctx = kc_ref.shape[1]
    for r in range(tq // ATTN_ROWS):
        rows = slice(r * ATTN_ROWS, (r + 1) * ATTN_ROWS)
        for kvh in range(ATTN_KV_HEADS):
            kc, kl, vc, vl = heads[kvh]
            base = kvh * GQA_GROUP * HEAD_DIM
            q = jnp.concatenate(
                [q_ref[0, rows, base + g * HEAD_DIM:base + (g + 1) * HEAD_DIM] for g in range(GQA_GROUP)], axis=0)
            s = jnp.concatenate([_dot_nt(q, kc), _dot_nt(q, kl)], axis=1)
            p = jnp.exp2(s - jnp.max(s, -1, keepdims=True)).astype(BF16)
            pv = _dot(p[:, 0:s_ctx], vc) + _dot(p[:, s_ctx:], vl)
            o = pv[:, 0:HEAD_DIM] / pv[:, HEAD_DIM:HEAD_DIM + 1]
            o_ref[0, rows, base:base + GQA_GROUP * HEAD_DIM] = jnp.concatenate(
                [o[g * ATTN_ROWS:(g + 1) * ATTN_ROWS] for g in range(GQA_GROUP)], axis=1).astype(BF16)


def _attention(q, kc, kl, vc, vl, *, tq):
    b, l, _ = q.shape
    s_ctx = kc.shape[1]
    ctx_map = lambda i, t: (i, 0, 0)
    return pl.pallas_call(
        functools.partial(_attn_kernel, tq=tq),
        grid=(b, l // tq),
        in_specs=[pl.BlockSpec((1, tq, D_ATTN), lambda i, t: (i, t, 0)),
                  pl.BlockSpec((1, s_ctx, D_KV), ctx_map),
                  pl.BlockSpec((1, l, D_KV), ctx_map),
                  pl.BlockSpec((1, s_ctx, D_KV), ctx_map),
                  pl.BlockSpec((1, l, D_KV), ctx_map)],
        out_specs=pl.BlockSpec((1, tq, D_ATTN), lambda i, t: (i, t, 0)),
        out_shape=jax.ShapeDtypeStruct((b, l, D_ATTN), BF16),
        compiler_params=_params("parallel", "parallel"),
        name="attention",
    )(q, kc, kl, vc, vl)


def _first_index_of_max(vals, lane):
    m = jnp.max(vals, -1, keepdims=True)
    idx = jnp.min(jnp.where(vals == m, lane, LANES), -1, keepdims=True)
    return m, idx


def _router(logits):
    lane = lax.broadcasted_iota(jnp.int32, logits.shape, 1)
    is_e = lane < N_EXPERTS
    is_g = (lane >= N_EXPERTS) & (lane < 2 * N_EXPERTS)
    gl = jnp.where(is_g, logits, NEG_BIG)
    g_max, g_lane = _first_index_of_max(gl, lane)
    g_sum = jnp.sum(jnp.where(is_g, jnp.exp(gl - g_max), 0.0), -1, keepdims=True) * (1.0 / EXPERTS_PER_GROUP)
    g_val = 1.0 / g_sum
    g_idx = (g_lane - N_EXPERTS) // EXPERTS_PER_GROUP
    in_group = is_e & ((lane // EXPERTS_PER_GROUP) == g_idx)
    el = jnp.where(in_group, logits, NEG_BIG)
    m1, i1 = _first_index_of_max(el, lane)
    el2 = jnp.where(lane == i1, NEG_BIG, el)
    m2, i2 = _first_index_of_max(el2, lane)
    r = jnp.exp(m2 - m1)
    w1 = 1.0 / (1.0 + r)
    w2 = r * w1
    return g_val * (jnp.where(lane == i1, w1, 0.0) + jnp.where(lane == i2, w2, 0.0)), g_idx


def _out_proj_kernel(y_ref, o_ref, x_ref, ga_ref, scf_ref, shf_ref, w_ref, g1_ref, b1_ref, wr_ref, br_ref,
                     x1_ref, t_ref, comb_ref, *, alpha):
    tm = x_ref.shape[1]
    n_sub = 2 if tm % (2 * SUBLANES * 2) == 0 else 1
    for r in range(n_sub):
        rows = slice(r * (tm // n_sub), (r + 1) * (tm // n_sub))
        mix = _dot(y_ref[0, rows, :], w_ref[0:D_SSD, :]) + _dot(o_ref[0, rows, :], w_ref[D_SSD:D_SSD + D_ATTN, :])
        x1 = _layer_norm(alpha * x_ref[0, rows, :] + ga_ref[0] * mix, g1_ref[...], b1_ref[...])
        x1_ref[0, rows, :] = x1
        t = x1 * (1.0 + scf_ref[0]) + shf_ref[0]
        t_hi, t_lo = _split2(t)
        t_ref[0, rows, :] = t_hi
        both = _dot(t_hi, wr_ref[...])
        logits = both[:, 0:LANES] + (both[:, LANES:2 * LANES] + _dot(t_lo, wr_ref[:, 0:LANES]))
        comb_ref[0, rows, :] = logits + br_ref[...]


def _out_proj(y, o, x, ga, scf, shf, w_out, g1, b1, wr, br, *, alpha, tm):
    b, l, d = x.shape
    row_map = lambda i, t: (i, t, 0)
    mod_map = lambda i, t: (i, 0, 0)
    const2 = lambda i, t: (0, 0)
    return pl.pallas_call(
        functools.partial(_out_proj_kernel, alpha=alpha),
        grid=(b, l // tm),
        in_specs=[pl.BlockSpec((1, tm, D_SSD), row_map),
                  pl.BlockSpec((1, tm, D_ATTN), row_map),
                  pl.BlockSpec((1, tm, d), row_map),
                  pl.BlockSpec((1, 1, d), mod_map),
                  pl.BlockSpec((1, 1, d), mod_map),
                  pl.BlockSpec((1, 1, d), mod_map),
                  pl.BlockSpec((D_SSD + D_ATTN, d), const2),
                  pl.BlockSpec((1, d), const2),
                  pl.BlockSpec((1, d), const2),
                  pl.BlockSpec((d, 2 * LANES), const2),
                  pl.BlockSpec((1, LANES), const2)],
        out_specs=[pl.BlockSpec((1, tm, d), row_map),
                   pl.BlockSpec((1, tm, d), row_map),
                   pl.BlockSpec((1, tm, LANES), row_map)],
        out_shape=[jax.ShapeDtypeStruct((b, l, d), F32),
                   jax.ShapeDtypeStruct((b, l, d), BF16),
                   jax.ShapeDtypeStruct((b, l, LANES), F32)],
        compiler_params=_params("parallel", "parallel"),
        name="out_proj",
    )(y, o, x, ga, scf, shf, w_out, g1, b1, wr, br)


MOE_ROWS = 128


def _moe_kernel(t_ref, comb_ref, x1_ref, gf_ref, wg_ref, wu_ref, wd_ref, g2_ref, b2_ref, out_ref,
                tc_ref, cw_ref, oc_ref, *, alpha):
    t = t_ref[0]
    tm = t.shape[0]
    comb, g_idx = _router(comb_ref[0])
    lane = lax.broadcasted_iota(jnp.int32, comb.shape, 1)

    def expert_block(rows_t, rows_w, g):
        hids = []
        for i in range(EXPERTS_PER_GROUP):
            e = g * EXPERTS_PER_GROUP + i
            ce = jnp.sum(jnp.where(lax.broadcasted_iota(jnp.int32, rows_w.shape, 1) == e, rows_w, 0.0), -1,
                         keepdims=True)
            hids.append((_silu(_dot(rows_t, wg_ref[e])) * _dot(rows_t, wu_ref[e]) * ce).astype(BF16))
        return _dot(jnp.concatenate(hids, axis=1), wd_ref[g])

    onehot = jnp.where(lane == g_idx, 1.0, 0.0)
    before = lax.broadcasted_iota(jnp.int32, (tm, tm), 1) <= lax.broadcasted_iota(jnp.int32, (tm, tm), 0)
    ranks = _dot(jnp.where(before, 1.0, 0.0).astype(BF16), onehot.astype(BF16))
    counts = ranks[tm - 1:tm, :]
    lane_row = lax.broadcasted_iota(jnp.int32, counts.shape, 1)
    sizes = [jnp.sum(jnp.where(lane_row == g, counts, 0.0)) for g in range(MOE_GROUPS)]
    starts = [0.0]
    for g in range(MOE_GROUPS - 1):
        starts.append(starts[-1] + sizes[g])
    start_of_own = sum(jnp.where(g_idx == g, starts[g], 0.0) for g in range(1, MOE_GROUPS))
    slot = start_of_own + jnp.sum(onehot * ranks, -1, keepdims=True) - 1.0

    slot_row = jnp.broadcast_to(slot, (tm, LANES)).T[0:1, :]
    to_slots = jnp.where(lax.broadcasted_iota(jnp.int32, (tm, tm), 0).astype(F32) == slot_row, 1.0, 0.0).astype(BF16)
    tc_ref[...] = _dot(to_slots, t).astype(BF16)
    c_hi, c_lo = _split2(comb)
    packed = jnp.where(lane < N_EXPERTS, c_hi, pltpu.roll(c_lo.astype(F32), N_EXPERTS, 1).astype(BF16))
    moved = _dot(to_slots, packed)
    cw_ref[...] = moved + pltpu.roll(moved, LANES - N_EXPERTS, 1)
    oc_ref[...] = jnp.zeros(oc_ref.shape, F32)

    for blk in range(tm // MOE_ROWS):
        rows = slice(blk * MOE_ROWS, (blk + 1) * MOE_ROWS)
        for g in range(MOE_GROUPS):
            @pl.when((starts[g] < (blk + 1) * MOE_ROWS) & (starts[g] + sizes[g] > blk * MOE_ROWS))
            def _():
                oc_ref[rows, :] += expert_block(tc_ref[rows, :], cw_ref[rows, :], g)

    from_slots = jnp.where(lax.broadcasted_iota(jnp.int32, (tm, tm), 1).astype(F32) == slot, 1.0, 0.0).astype(BF16)
    moe = _dot(from_slots, oc_ref[...].astype(BF16))
    out_ref[0] = _layer_norm(alpha * x1_ref[0] + gf_ref[0] * moe, g2_ref[...], b2_ref[...])


def _moe(t, comb, x1, gf, wg, wu, wd, g2, b2, *, alpha, tm):
    b, l, d = x1.shape
    row_map = lambda i, r: (i, r, 0)
    const2 = lambda i, r: (0, 0)
    const3 = lambda i, r: (0, 0, 0)
    resident = pl.Buffered(1)
    return pl.pallas_call(
        functools.partial(_moe_kernel, alpha=alpha),
        grid=(b, l // tm),
        in_specs=[pl.BlockSpec((1, tm, d), row_map),
                  pl.BlockSpec((1, tm, LANES), row_map),
                  pl.BlockSpec((1, tm, d), row_map),
                  pl.BlockSpec((1, 1, d), lambda i, r: (i, 0, 0)),
                  pl.BlockSpec(wg.shape, const3, pipeline_mode=resident),
                  pl.BlockSpec(wu.shape, const3, pipeline_mode=resident),
                  pl.BlockSpec(wd.shape, const3, pipeline_mode=resident),
                  pl.BlockSpec((1, d), const2),
                  pl.BlockSpec((1, d), const2)],
        out_specs=pl.BlockSpec((1, tm, d), row_map),
        out_shape=jax.ShapeDtypeStruct((b, l, d), F32),
        scratch_shapes=[pltpu.VMEM((tm, d), BF16),
                        pltpu.VMEM((tm, LANES), F32),
                        pltpu.VMEM((tm, d), F32)],
        compiler_params=_params("parallel", "parallel"),
        name="moe",
    )(t, comb, x1, gf, wg, wu, wd, g2, b2)


def _rope_tables(n_rows):
    rows = jnp.broadcast_to(jnp.arange(n_rows, dtype=F32)[:, None], (n_rows, GRID_W)).reshape(-1)
    cols = jnp.broadcast_to(jnp.arange(GRID_W, dtype=F32)[None, :], (n_rows, GRID_W)).reshape(-1)
    axis_dim = HEAD_DIM // 2
    inv_freq = jnp.power(ROPE_THETA, -jnp.arange(0, axis_dim, 2, dtype=F32) / axis_dim)
    ang = jnp.concatenate([rows[:, None] * inv_freq, cols[:, None] * inv_freq], -1)
    cos, sin = jnp.cos(ang), jnp.sin(ang)
    cos_h = jnp.concatenate([cos, cos], -1)
    sin_h = jnp.concatenate([-sin, sin], -1)
    reps = LANES // HEAD_DIM
    return jnp.tile(cos_h, (1, reps)), jnp.tile(sin_h, (1, reps))


def _head_perm(n_heads):
    within = jnp.concatenate([jnp.arange(0, HEAD_DIM, 2), jnp.arange(1, HEAD_DIM, 2)])
    return (jnp.arange(n_heads)[:, None] * HEAD_DIM + within[None, :]).reshape(-1)


def kernel(x, c, ctx, c_ctx, w_mod, b_mod, w_in, conv_w, conv_b, dt_bias, a_log, d_skip, ssd_norm_g, q_norm_g,
           k_norm_g, w_out, ln1_g, ln1_b, w_rg, b_rg, w_re, b_re, w_gate, w_up, w_down, ln2_g, ln2_b):
    depth = w_in.shape[0]
    assert depth == 1, "single-layer block only"
    b, l, d = x.shape
    s_ctx = ctx.shape[1]
    assert d == D_MODEL and l % CHUNK == 0 and s_ctx % CHUNK == 0 and l % GRID_W == 0
    alpha = (2.0 * depth) ** 0.25

    m_rows = -(-(b + 1) // SUBLANES) * SUBLANES
    c_rows = jnp.zeros((m_rows, d), F32).at[:b].set(c).at[b].set(c_ctx)
    mod = _modulation(c_rows, w_mod[0], b_mod[0])
    sh_a, sc_a, g_a, sh_f, sc_f, g_f = [mod[:b, i * d:(i + 1) * d].reshape(b, 1, d) for i in range(6)]
    sh_c = mod[b:b + 1, 0:d].reshape(1, 1, d)
    sc_c = mod[b:b + 1, d:2 * d].reshape(1, 1, d)

    w = w_in[0]
    perm_q, perm_k = _head_perm(ATTN_Q_HEADS), _head_perm(ATTN_KV_HEADS)
    w_dt = jnp.pad(w[:, OFF_DT:OFF_K], ((0, 0), (0, DT_PAD - 2 * SSD_HEADS)))
    w_full = jnp.concatenate([w[:, :OFF_Q], w[:, OFF_Q:OFF_XBC][:, perm_q], w[:, OFF_XBC:OFF_DT],
                              w[:, OFF_K:OFF_V][:, perm_k], w[:, OFF_V:], w_dt], axis=1).astype(BF16)
    w_side = w_full[:, OFF_XBC:]
    within = perm_k[:HEAD_DIM]
    gq = jnp.tile(q_norm_g[0][within], ATTN_Q_HEADS).reshape(1, D_ATTN) * (HEAD_DIM ** -0.5 * math.log2(math.e))
    gk = jnp.tile(k_norm_g[0][within], ATTN_KV_HEADS).reshape(1, D_KV)
    lane = jnp.arange(MXU_DIM)
    ones_bd = (lane[:, None] // HEAD_DIM == lane[None, :] // HEAD_DIM).astype(BF16)
    cos, sin = _rope_tables(l // GRID_W)

    convw, convb = conv_w[0], conv_b[0].reshape(1, D_XBC)
    tm_in = 512 if l % 512 == 0 else CHUNK
    act_c, k_c, v_c, dt_c = _in_proj(ctx, sc_c, sh_c, w_side, ones_bd, gq, gk, convw, convb, None, None,
                                     with_zq=False, tm=s_ctx)
    z, q, act, k, v, dt = _in_proj(x, sc_a, sh_a, w_full, ones_bd, gq, gk, convw, convb, cos, sin,
                                   with_zq=True, tm=tm_in)

    pad_dt = lambda u: jnp.pad(u.reshape(1, 2 * SSD_HEADS).astype(F32), ((0, 0), (0, DT_PAD - 2 * SSD_HEADS)))
    dtb = pad_dt(dt_bias[0])
    a_neg = pad_dt(-jnp.exp(a_log[0].astype(F32)))
    skip = jnp.repeat(d_skip[0].astype(F32), SSD_HEAD_DIM).reshape(1, D_SSD)
    ng = ssd_norm_g[0].reshape(1, D_SSD)
    h_ctx = _ssd_ctx(act_c, dt_c, dtb, a_neg)
    y_ssd = _ssd(act, dt, z, h_ctx, dtb, a_neg, skip, ng)

    o = _attention(q, k_c, k, v_c, v, tq=512 if l % 512 == 0 else ATTN_ROWS)

    wr = jnp.concatenate([w_re[0], jnp.repeat(w_rg[0], EXPERTS_PER_GROUP, axis=1)], axis=1)
    wr = jnp.pad(wr, ((0, 0), (0, LANES - 2 * N_EXPERTS)))
    wr_hi = wr.astype(BF16)
    wr = jnp.concatenate([wr_hi, (wr - wr_hi.astype(F32)).astype(BF16)], axis=1)
    br = jnp.concatenate([b_re[0], jnp.repeat(b_rg[0], EXPERTS_PER_GROUP)])
    br = jnp.pad(br, (0, LANES - 2 * N_EXPERTS)).reshape(1, LANES)
    x1, t, comb = _out_proj(y_ssd, o, x, g_a, sc_f, sh_f, w_out[0].astype(BF16), ln1_g[0].reshape(1, d),
                            ln1_b[0].reshape(1, d), wr, br, alpha=alpha, tm=tm_in)

    wd = w_down[0].astype(BF16).reshape(MOE_GROUPS, EXPERTS_PER_GROUP * D_EXPERT, d)
    return _moe(t, comb, x1, g_f, w_gate[0].astype(BF16), w_up[0].astype(BF16), wd, ln2_g[0].reshape(1, d),
                ln2_b[0].reshape(1, d), alpha=alpha, tm=tm_in)
```

```python
import functools
import math

import jax
import jax.numpy as jnp
from jax import lax
from jax.experimental import pallas as pl
from jax.experimental.pallas import tpu as pltpu

F32 = jnp.float32
BF16 = jnp.bfloat16

D_MODEL = 1024
GRID_W = 64
SSD_HEADS = 16
SSD_HEAD_DIM = 64
D_SSD = SSD_HEADS * SSD_HEAD_DIM
SSD_GROUPS = 2
SSD_HEADS_PER_GROUP = SSD_HEADS // SSD_GROUPS
D_STATE = 128
CONV_W = 5
CHUNK = 128
D_BC = SSD_GROUPS * D_STATE
D_XBC = D_SSD + 2 * D_BC
ATTN_Q_HEADS = 16
ATTN_KV_HEADS = 4
HEAD_DIM = 64
GQA_GROUP = ATTN_Q_HEADS // ATTN_KV_HEADS
D_ATTN = ATTN_Q_HEADS * HEAD_DIM
D_KV = ATTN_KV_HEADS * HEAD_DIM
ROPE_THETA = 10000.0
MOE_GROUPS = 4
EXPERTS_PER_GROUP = 4
N_EXPERTS = MOE_GROUPS * EXPERTS_PER_GROUP
D_EXPERT = 256

OFF_Q = D_SSD
OFF_XBC = OFF_Q + D_ATTN
OFF_DT = OFF_XBC + D_XBC
OFF_K = OFF_DT + 2 * SSD_HEADS
OFF_V = OFF_K + D_KV

LANES = 128
SUBLANES = 8
MXU_DIM = 256
VMEM_LIMIT = 56 * 1024 * 1024

DT_PAD = LANES
HALO = SUBLANES
NEG_BIG = -1e30


def _dot(a, b):
    return jnp.dot(a, b, preferred_element_type=F32)


def _dot_nt(a, b):
    return lax.dot_general(a, b, (((1,), (1,)), ((), ())), preferred_element_type=F32)


def _split2(a):
    hi = a.astype(BF16)
    lo = (a - hi.astype(F32)).astype(BF16)
    return hi, lo


def _split3(a):
    hi = a.astype(BF16)
    r = a - hi.astype(F32)
    mid = r.astype(BF16)
    lo = (r - mid.astype(F32)).astype(BF16)
    return hi, mid, lo


def _dot_f32(a, b):
    a_hi, a_lo = _split2(a)
    b_hi, b_lo = _split2(b)
    return _dot(a_hi, b_hi) + (_dot(a_hi, b_lo) + _dot(a_lo, b_hi))


def _silu(x):
    return x / (1.0 + jnp.exp(-x))


def _softplus(x):
    return jnp.maximum(x, 0.0) + jnp.log(1.0 + jnp.exp(-jnp.abs(x)))


def _layer_norm(x, g, b):
    mu = jnp.mean(x, -1, keepdims=True)
    xc = x - mu
    var = jnp.mean(xc * xc, -1, keepdims=True)
    return xc * lax.rsqrt(var + 1e-5) * g + b


def _params(*sem):
    return pltpu.CompilerParams(dimension_semantics=sem, vmem_limit_bytes=VMEM_LIMIT)


def _mod_kernel(c_ref, w_ref, b_ref, o_ref):
    o_ref[...] = _dot_f32(_silu(c_ref[...]), w_ref[...]) + b_ref[...]


def _modulation(c_rows, w_mod, b_mod):
    m, d = c_rows.shape
    n = w_mod.shape[1]
    tn = 512
    return pl.pallas_call(
        _mod_kernel,
        grid=(n // tn,),
        in_specs=[pl.BlockSpec((m, d), lambda j: (0, 0)),
                  pl.BlockSpec((d, tn), lambda j: (0, j)),
                  pl.BlockSpec((1, tn), lambda j: (0, j))],
        out_specs=pl.BlockSpec((m, tn), lambda j: (0, j)),
        out_shape=jax.ShapeDtypeStruct((m, n), F32),
        compiler_params=_params("parallel"),
        name="modulation",
    )(c_rows, w_mod, b_mod.reshape(1, n))


def _head_sumsq(x2, ones_bd):
    hi, lo = _split2(x2)
    blocks = []
    for j in range(x2.shape[1] // MXU_DIM):
        sl = slice(j * MXU_DIM, (j + 1) * MXU_DIM)
        blocks.append(_dot(hi[:, sl], ones_bd) + _dot(lo[:, sl], ones_bd))
    return blocks[0] if len(blocks) == 1 else jnp.concatenate(blocks, axis=1)


def _rope_partner(x):
    n = x.shape[1]
    half = HEAD_DIM // 2
    lane = lax.broadcasted_iota(jnp.int32, x.shape, 1)
    first_half = (lane & (HEAD_DIM - 1)) < half
    return jnp.where(first_half, pltpu.roll(x, n - half, 1), pltpu.roll(x, half, 1))


def _norm_rope(x, gain, ones_bd, cos, sin):
    ss = _head_sumsq(x * x, ones_bd)
    xn = x * lax.rsqrt(ss * (1.0 / HEAD_DIM) + 1e-6) * gain
    if cos is not None:
        reps = x.shape[1] // cos.shape[1]
        c = jnp.concatenate([cos] * reps, axis=1) if reps > 1 else cos
        s = jnp.concatenate([sin] * reps, axis=1) if reps > 1 else sin
        xn = xn * c + _rope_partner(xn) * s
    return xn


def _in_proj_kernel(*refs, with_zq, with_rope, tm):
    it = iter(refs)
    x_ref, xprev_ref, xnext_ref, sc_ref, sh_ref, w_ref, ones_ref, gq_ref, gk_ref, convw_ref, convb_ref = (
        next(it) for _ in range(11))
    cos_ref = sin_ref = None
    if with_rope:
        cos_ref, sin_ref = next(it), next(it)
    if with_zq:
        z_ref, q_ref = next(it), next(it)
    xbc_ref, k_ref, v_ref, dt_ref, pad_ref = (next(it) for _ in range(5))

    t = pl.program_id(0)
    scale, shift = 1.0 + sc_ref[0], sh_ref[0]
    hx = (x_ref[0] * scale + shift).astype(BF16)
    ones_bd = ones_ref[...]
    cos = cos_ref[...] if with_rope else None
    sin = sin_ref[...] if with_rope else None
    off = 0
    if with_zq:
        z_ref[0] = _dot(hx, w_ref[:, 0:D_SSD]).astype(BF16)
        q = _dot(hx, w_ref[:, D_SSD:D_SSD + D_ATTN])
        q_ref[0] = _norm_rope(q, gq_ref[...], ones_bd, cos, sin).astype(BF16)
        off = D_SSD + D_ATTN
    x_ext = jnp.concatenate([xprev_ref[0], x_ref[0], xnext_ref[0]], axis=0)
    raw = _dot((x_ext * scale + shift).astype(BF16), w_ref[:, off:off + D_XBC])
    row = lax.broadcasted_iota(jnp.int32, (tm + 2 * HALO, 1), 0)
    inside = ((row >= HALO) | (t > 0)) & ((row < tm + HALO) | (t < pl.num_programs(0) - 1))
    pad_ref[...] = jnp.where(inside, raw, 0.0)
    acc = convb_ref[...]
    for j in range(CONV_W):
        start = HALO - CONV_W // 2 + j
        acc = acc + convw_ref[j:j + 1, :] * pad_ref[start:start + tm, :]
    xbc_ref[0] = _silu(acc).astype(BF16)
    off += D_XBC
    k = _dot(hx, w_ref[:, off:off + D_KV])
    k_ref[0] = _norm_rope(k, gk_ref[...], ones_bd, cos, sin).astype(BF16)
    off += D_KV
    v_ref[0] = _dot(hx, w_ref[:, off:off + D_KV]).astype(BF16)
    off += D_KV
    dt_ref[0] = _dot(hx, w_ref[:, off:off + DT_PAD])


def _in_proj(x, scale, shift, w, ones_bd, gq, gk, convw, convb, cos, sin, *, with_zq, tm):
    b, l, d = x.shape
    nw = w.shape[1]
    with_rope = cos is not None
    per_batch = scale.shape[0] == b and b > 1
    mod_map = (lambda t, i: (i, 0, 0)) if per_batch else (lambda t, i: (0, 0, 0))
    row_map = lambda t, i: (i, t, 0)
    const2 = lambda t, i: (0, 0)
    per = tm // HALO
    in_specs = [pl.BlockSpec((1, tm, d), row_map),
                pl.BlockSpec((1, HALO, d), lambda t, i: (i, jnp.maximum(t * per - 1, 0), 0)),
                pl.BlockSpec((1, HALO, d), lambda t, i: (i, jnp.minimum((t + 1) * per, l // HALO - 1), 0)),
                pl.BlockSpec((1, 1, d), mod_map),
                pl.BlockSpec((1, 1, d), mod_map),
                pl.BlockSpec((d, nw), const2),
                pl.BlockSpec((MXU_DIM, MXU_DIM), const2),
                pl.BlockSpec((1, D_ATTN), const2),
                pl.BlockSpec((1, D_KV), const2),
                pl.BlockSpec((CONV_W, D_XBC), const2),
                pl.BlockSpec((1, D_XBC), const2)]
    args = [x, x, x, scale, shift, w, ones_bd, gq, gk, convw, convb]
    if with_rope:
        in_specs += [pl.BlockSpec((tm, cos.shape[1]), lambda t, i: (t, 0))] * 2
        args += [cos, sin]
    out_shape, out_specs = [], []

    def add_out(width, dtype):
        out_shape.append(jax.ShapeDtypeStruct((b, l, width), dtype))
        out_specs.append(pl.BlockSpec((1, tm, width), row_map))

    if with_zq:
        add_out(D_SSD, BF16)
        add_out(D_ATTN, BF16)
    add_out(D_XBC, BF16)
    add_out(D_KV, BF16)
    add_out(D_KV, BF16)
    add_out(DT_PAD, F32)
    return pl.pallas_call(
        functools.partial(_in_proj_kernel, with_zq=with_zq, with_rope=with_rope, tm=tm),
        grid=(l // tm, b),
        in_specs=in_specs,
        out_specs=out_specs,
        out_shape=out_shape,
        scratch_shapes=[pltpu.VMEM((tm + 2 * HALO, D_XBC), F32)],
        compiler_params=_params("parallel", "parallel"),
        name="in_proj_latent" if with_zq else "in_proj_ctx",
    )(*args)


PAIR = 2 * SSD_HEAD_DIM
PAIRS_PER_GROUP = SSD_HEADS_PER_GROUP // 2
D_GROUP = SSD_HEADS_PER_GROUP * SSD_HEAD_DIM


def _chunk_tables(dt_raw, dtb, a_neg, direction):
    dtv = _softplus(dt_raw + dtb)
    qi = lax.broadcasted_iota(jnp.int32, (CHUNK, CHUNK), 0)
    si = lax.broadcasted_iota(jnp.int32, (CHUNK, CHUNK), 1)
    mask = (si <= qi) if direction == 0 else (si >= qi)
    tri = jnp.where(mask, 1.0, 0.0).astype(BF16)
    hi, mid, lo = _split3(dtv * a_neg)
    a = _dot(tri, hi) + (_dot(tri, mid) + _dot(tri, lo))
    a_t = a.T
    dtv_t = dtv.T
    last = CHUNK - 1 if direction == 0 else 0
    tot = a_t[:, last:last + 1]
    log2e = math.log2(math.e)
    return dict(a=a * log2e, mask=mask, rowp_t=a_t * log2e - jnp.log2(dtv_t), f_t=dtv_t * jnp.exp(tot - a_t),
                etot_t=jnp.exp(tot))


def _ssd_chunk_dir(act, tables, direction, h_ref, want_y):
    a, mask, rowp_t, f_t, etot_t = (tables[k] for k in ("a", "mask", "rowp_t", "f_t", "etot_t"))
    lane = lax.broadcasted_iota(jnp.int32, (CHUNK, PAIR), 1)
    low = lane < SSD_HEAD_DIM
    zero = jnp.zeros((CHUNK, PAIR), BF16)
    ys = []
    for g in range(SSD_GROUPS):
        bg = act[:, D_SSD + g * D_STATE:D_SSD + (g + 1) * D_STATE]
        bg_t = bg.astype(F32).T
        if want_y:
            cg = act[:, D_SSD + D_BC + g * D_STATE:D_SSD + D_BC + (g + 1) * D_STATE]
            cb = _dot_nt(cg, bg)
            y_off = _dot(cg, h_ref[g].astype(BF16))
        for j in range(PAIRS_PER_GROUP):
            col0 = (g * PAIRS_PER_GROUP + j) * PAIR
            xp = act[:, col0:col0 + PAIR]
            x_bd = jnp.concatenate([jnp.where(low, xp, zero), jnp.where(low, zero, xp)], axis=0)
            m_parts, s_parts, e_parts, etots = [], [], [], []
            for u in range(2):
                hc = direction * SSD_HEADS + g * SSD_HEADS_PER_GROUP + 2 * j + u
                s_parts.append((bg_t * f_t[hc:hc + 1, :]).astype(BF16))
                etots.append(jnp.broadcast_to(etot_t[hc:hc + 1, :], (D_STATE, PAIR)))
                if want_y:
                    col = jnp.broadcast_to(a[:, hc:hc + 1], (CHUNK, CHUNK))
                    decay = jnp.exp2(jnp.where(mask, col - rowp_t[hc:hc + 1, :], NEG_BIG))
                    m_parts.append((cb * decay).astype(BF16))
                    e_parts.append(jnp.exp2(col))
            hs = slice(j * PAIR, (j + 1) * PAIR)
            new_state = jnp.where(low, etots[0], etots[1]) * h_ref[g, :, hs] + _dot(
                jnp.concatenate(s_parts, axis=1), x_bd)
            if want_y:
                y_diag = _dot(jnp.concatenate(m_parts, axis=1), x_bd)
                ys.append(y_diag + y_off[:, hs] * jnp.where(low, e_parts[0], e_parts[1]))
            h_ref[g, :, hs] = new_state
    return jnp.concatenate(ys, axis=1) if want_y else None


def _ssd_ctx_kernel(act_ref, dt_ref, dtb_ref, a_ref, hout_ref, h_ref, *, n_chunks):
    dtb, a_neg = dtb_ref[...], a_ref[...]
    for direction in range(2):
        h_ref[...] = jnp.zeros(h_ref.shape, F32)
        order = range(n_chunks) if direction == 0 else range(n_chunks - 1, -1, -1)
        for c in order:
            rows = slice(c * CHUNK, (c + 1) * CHUNK)
            tables = _chunk_tables(dt_ref[0, rows, :], dtb, a_neg, direction)
            _ssd_chunk_dir(act_ref[0, rows, :], tables, direction, h_ref, False)
        hout_ref[0, direction] = h_ref[...]


def _ssd_ctx(act, dt, dtb, a_neg):
    b, s, _ = act.shape
    n_chunks = s // CHUNK
    const2 = lambda i: (0, 0)
    state = (SSD_GROUPS, D_STATE, D_GROUP)
    return pl.pallas_call(
        functools.partial(_ssd_ctx_kernel, n_chunks=n_chunks),
        grid=(b,),
        in_specs=[pl.BlockSpec((1, s, D_XBC), lambda i: (i, 0, 0)),
                  pl.BlockSpec((1, s, DT_PAD), lambda i: (i, 0, 0)),
                  pl.BlockSpec((1, DT_PAD), const2),
                  pl.BlockSpec((1, DT_PAD), const2)],
        out_specs=pl.BlockSpec((1, 2) + state, lambda i: (i, 0, 0, 0, 0)),
        out_shape=jax.ShapeDtypeStruct((b, 2) + state, F32),
        scratch_shapes=[pltpu.VMEM(state, F32)],
        compiler_params=_params("parallel"),
        name="ssd_ctx",
    )(act, dt, dtb, a_neg)


def _ssd_kernel(act_ref, dt_ref, z_ref, h0_ref, dtb_ref, a_ref, skip_ref, ng_ref, out_ref, y_ref, h_ref, *, n_chunks):
    j = pl.program_id(1)
    dtb, a_neg = dtb_ref[...], a_ref[...]

    @pl.when(j < n_chunks)
    def _forward():
        @pl.when(j == 0)
        def _():
            h_ref[...] = h0_ref[0, 0]

        act = act_ref[0]
        y = _ssd_chunk_dir(act, _chunk_tables(dt_ref[0], dtb, a_neg, 0), 0, h_ref, True)
        rows = pl.ds(pl.multiple_of(j * CHUNK, CHUNK), CHUNK)
        y_ref[rows, :] = y + act[:, 0:D_SSD].astype(F32) * skip_ref[...]

    @pl.when(j >= n_chunks)
    def _backward():
        @pl.when(j == n_chunks)
        def _():
            h_ref[...] = h0_ref[0, 1]

        c = 2 * n_chunks - 1 - j
        rows = pl.ds(pl.multiple_of(c * CHUNK, CHUNK), CHUNK)
        y = y_ref[rows, :] + _ssd_chunk_dir(act_ref[0], _chunk_tables(dt_ref[0], dtb, a_neg, 1), 1, h_ref, True)
        y = y * _silu(z_ref[0].astype(F32))
        parts = []
        for g in range(SSD_GROUPS):
            yg = y[:, g * D_GROUP:(g + 1) * D_GROUP]
            parts.append(yg * lax.rsqrt(jnp.mean(yg * yg, -1, keepdims=True) + 1e-6))
        out_ref[0] = (jnp.concatenate(parts, axis=1) * ng_ref[...]).astype(BF16)


def _ssd(act, dt, z, h0, dtb, a_neg, skip, ng):
    b, l, _ = act.shape
    nc = l // CHUNK

    def chunk_of(j):
        return jnp.where(j < nc, j, 2 * nc - 1 - j)

    def bwd_chunk(j):
        return jnp.where(j < nc, nc - 1, 2 * nc - 1 - j)

    const2 = lambda i, j: (0, 0)
    state = (SSD_GROUPS, D_STATE, D_GROUP)
    return pl.pallas_call(
        functools.partial(_ssd_kernel, n_chunks=nc),
        grid=(b, 2 * nc),
        in_specs=[pl.BlockSpec((1, CHUNK, D_XBC), lambda i, j: (i, chunk_of(j), 0)),
                  pl.BlockSpec((1, CHUNK, DT_PAD), lambda i, j: (i, chunk_of(j), 0)),
                  pl.BlockSpec((1, CHUNK, D_SSD), lambda i, j: (i, bwd_chunk(j), 0)),
                  pl.BlockSpec((1, 2) + state, lambda i, j: (i, 0, 0, 0, 0)),
                  pl.BlockSpec((1, DT_PAD), const2),
                  pl.BlockSpec((1, DT_PAD), const2),
                  pl.BlockSpec((1, D_SSD), const2),
                  pl.BlockSpec((1, D_SSD), const2)],
        out_specs=pl.BlockSpec((1, CHUNK, D_SSD), lambda i, j: (i, bwd_chunk(j), 0)),
        out_shape=jax.ShapeDtypeStruct((b, l, D_SSD), BF16),
        scratch_shapes=[pltpu.VMEM((l, D_SSD), F32),
                        pltpu.VMEM(state, F32)],
        compiler_params=_params("parallel", "arbitrary"),
        name="ssd",
    )(act, dt, z, h0, dtb, a_neg, skip, ng)


ATTN_ROWS = 128


def _attn_kernel(q_ref, kc_ref, kl_ref, vc_ref, vl_ref, o_ref, k_s, v_s, *, tq):
    s_ctx = kc_ref.shape[1]

    @pl.when(pl.program_id(1) == 0)
    def _regroup():
        def with_ones(v):
            lane = lax.broadcasted_iota(jnp.int32, (v.shape[0], LANES - HEAD_DIM), 1)
            return jnp.concatenate([v, jnp.where(lane == 0, 1.0, 0.0).astype(BF16)], axis=1)

        for kvh in range(ATTN_KV_HEADS):
            cols = slice(kvh * HEAD_DIM, (kvh + 1) * HEAD_DIM)
            k_s[kvh, 0:s_ctx, :] = kc_ref[0, :, cols]
            k_s[kvh, s_ctx:, :] = kl_ref[0, :, cols]
            v_s[kvh, 0:s_ctx, :] = with_ones(vc_ref[0, :, cols])
            v_s[kvh, s_ctx:, :] = with_ones(vl_ref[0, :, cols])

    for r in range(tq // ATTN_ROWS):
        rows = slice(r * ATTN_ROWS, (r + 1) * ATTN_ROWS)
        for kvh in range(ATTN_KV_HEADS):
            base = kvh * GQA_GROUP * HEAD_DIM
            q = jnp.concatenate(
                [q_ref[0, rows, base + g * HEAD_DIM:base + (g + 1) * HEAD_DIM] for g in range(GQA_GROUP)], axis=0)
            s = _dot_nt(q, k_s[kvh])
            p = jnp.exp2(s - jnp.max(s, -1, keepdims=True)).astype(BF16)
            pv = _dot(p, v_s[kvh])
            o = pv[:, 0:HEAD_DIM] / pv[:, HEAD_DIM:HEAD_DIM + 1]
            o_ref[0, rows, base:base + GQA_GROUP * HEAD_DIM] = jnp.concatenate(
                [o[g * ATTN_ROWS:(g + 1) * ATTN_ROWS] for g in range(GQA_GROUP)], axis=1).astype(BF16)


def _attention(q, kc, kl, vc, vl, *, tq):
    b, l, _ = q.shape
    s_ctx = kc.shape[1]
    ctx_map = lambda i, t: (i, 0, 0)
    return pl.pallas_call(
        functools.partial(_attn_kernel, tq=tq),
        grid=(b, l // tq),
        in_specs=[pl.BlockSpec((1, tq, D_ATTN), lambda i, t: (i, t, 0)),
                  pl.BlockSpec((1, s_ctx, D_KV), ctx_map),
                  pl.BlockSpec((1, l, D_KV), ctx_map),
                  pl.BlockSpec((1, s_ctx, D_KV), ctx_map),
                  pl.BlockSpec((1, l, D_KV), ctx_map)],
        out_specs=pl.BlockSpec((1, tq, D_ATTN), lambda i, t: (i, t, 0)),
        out_shape=jax.ShapeDtypeStruct((b, l, D_ATTN), BF16),
        scratch_shapes=[pltpu.VMEM((ATTN_KV_HEADS, s_ctx + l, HEAD_DIM), BF16),
                        pltpu.VMEM((ATTN_KV_HEADS, s_ctx + l, LANES), BF16)],
        compiler_params=_params("parallel", "arbitrary"),
        name="attention",
    )(q, kc, kl, vc, vl)


def _first_index_of_max(vals, lane):
    m = jnp.max(vals, -1, keepdims=True)
    idx = jnp.min(jnp.where(vals == m, lane, LANES), -1, keepdims=True)
    return m, idx


def _router(logits):
    lane = lax.broadcasted_iota(jnp.int32, logits.shape, 1)
    is_e = lane < N_EXPERTS
    is_g = (lane >= N_EXPERTS) & (lane < 2 * N_EXPERTS)
    gl = jnp.where(is_g, logits, NEG_BIG)
    g_max, g_lane = _first_index_of_max(gl, lane)
    g_sum = jnp.sum(jnp.where(is_g, jnp.exp(gl - g_max), 0.0), -1, keepdims=True) * (1.0 / EXPERTS_PER_GROUP)
    g_val = 1.0 / g_sum
    g_idx = (g_lane - N_EXPERTS) // EXPERTS_PER_GROUP
    in_group = is_e & ((lane // EXPERTS_PER_GROUP) == g_idx)
    el = jnp.where(in_group, logits, NEG_BIG)
    m1, i1 = _first_index_of_max(el, lane)
    el2 = jnp.where(lane == i1, NEG_BIG, el)
    m2, i2 = _first_index_of_max(el2, lane)
    r = jnp.exp(m2 - m1)
    w1 = 1.0 / (1.0 + r)
    w2 = r * w1
    return g_val * (jnp.where(lane == i1, w1, 0.0) + jnp.where(lane == i2, w2, 0.0)), g_idx


def _out_proj_kernel(y_ref, o_ref, x_ref, ga_ref, scf_ref, shf_ref, w_ref, g1_ref, b1_ref, wr_ref, br_ref,
                     x1_ref, t_ref, comb_ref, *, alpha):
    tm = x_ref.shape[1]
    n_sub = 2 if tm % (2 * SUBLANES * 2) == 0 else 1
    for r in range(n_sub):
        rows = slice(r * (tm // n_sub), (r + 1) * (tm // n_sub))
        mix = _dot(y_ref[0, rows, :], w_ref[0:D_SSD, :]) + _dot(o_ref[0, rows, :], w_ref[D_SSD:D_SSD + D_ATTN, :])
        x1 = _layer_norm(alpha * x_ref[0, rows, :] + ga_ref[0] * mix, g1_ref[...], b1_ref[...])
        x1_ref[0, rows, :] = x1
        t = x1 * (1.0 + scf_ref[0]) + shf_ref[0]
        t_hi, t_lo = _split2(t)
        t_ref[0, rows, :] = t_hi
        both = _dot(t_hi, wr_ref[...])
        logits = both[:, 0:LANES] + (both[:, LANES:2 * LANES] + _dot(t_lo, wr_ref[:, 0:LANES]))
        comb_ref[0, rows, :] = logits + br_ref[...]


def _out_proj(y, o, x, ga, scf, shf, w_out, g1, b1, wr, br, *, alpha, tm):
    b, l, d = x.shape
    row_map = lambda i, t: (i, t, 0)
    mod_map = lambda i, t: (i, 0, 0)
    const2 = lambda i, t: (0, 0)
    return pl.pallas_call(
        functools.partial(_out_proj_kernel, alpha=alpha),
        grid=(b, l // tm),
        in_specs=[pl.BlockSpec((1, tm, D_SSD), row_map),
                  pl.BlockSpec((1, tm, D_ATTN), row_map),
                  pl.BlockSpec((1, tm, d), row_map),
                  pl.BlockSpec((1, 1, d), mod_map),
                  pl.BlockSpec((1, 1, d), mod_map),
                  pl.BlockSpec((1, 1, d), mod_map),
                  pl.BlockSpec((D_SSD + D_ATTN, d), const2),
                  pl.BlockSpec((1, d), const2),
                  pl.BlockSpec((1, d), const2),
                  pl.BlockSpec((d, 2 * LANES), const2),
                  pl.BlockSpec((1, LANES), const2)],
        out_specs=[pl.BlockSpec((1, tm, d), row_map),
                   pl.BlockSpec((1, tm, d), row_map),
                   pl.BlockSpec((1, tm, LANES), row_map)],
        out_shape=[jax.ShapeDtypeStruct((b, l, d), F32),
                   jax.ShapeDtypeStruct((b, l, d), BF16),
                   jax.ShapeDtypeStruct((b, l, LANES), F32)],
        compiler_params=_params("parallel", "parallel"),
        name="out_proj",
    )(y, o, x, ga, scf, shf, w_out, g1, b1, wr, br)


MOE_ROWS = 128


def _moe_kernel(t_ref, comb_ref, x1_ref, gf_ref, wg_ref, wu_ref, wd_ref, g2_ref, b2_ref, out_ref,
                tc_ref, cw_ref, oc_ref, *, alpha):
    t = t_ref[0]
    tm = t.shape[0]
    comb, g_idx = _router(comb_ref[0])
    lane = lax.broadcasted_iota(jnp.int32, comb.shape, 1)

    def expert_block(rows_t, rows_w, g):
        hids = []
        for i in range(EXPERTS_PER_GROUP):
            e = g * EXPERTS_PER_GROUP + i
            ce = jnp.sum(jnp.where(lax.broadcasted_iota(jnp.int32, rows_w.shape, 1) == e, rows_w, 0.0), -1,
                         keepdims=True)
            hids.append((_silu(_dot(rows_t, wg_ref[e])) * _dot(rows_t, wu_ref[e]) * ce).astype(BF16))
        return _dot(jnp.concatenate(hids, axis=1), wd_ref[g])

    onehot = jnp.where(lane == g_idx, 1.0, 0.0)
    before = lax.broadcasted_iota(jnp.int32, (tm, tm), 1) <= lax.broadcasted_iota(jnp.int32, (tm, tm), 0)
    ranks = _dot(jnp.where(before, 1.0, 0.0).astype(BF16), onehot.astype(BF16))
    counts = ranks[tm - 1:tm, :]
    lane_row = lax.broadcasted_iota(jnp.int32, counts.shape, 1)
    sizes = [jnp.sum(jnp.where(lane_row == g, counts, 0.0)) for g in range(MOE_GROUPS)]
    starts = [0.0]
    for g in range(MOE_GROUPS - 1):
        starts.append(starts[-1] + sizes[g])
    start_of_own = sum(jnp.where(g_idx == g, starts[g], 0.0) for g in range(1, MOE_GROUPS))
    slot = start_of_own + jnp.sum(onehot * ranks, -1, keepdims=True) - 1.0

    slot_row = jnp.broadcast_to(slot, (tm, LANES)).T[0:1, :]
    to_slots = jnp.where(lax.broadcasted_iota(jnp.int32, (tm, tm), 0).astype(F32) == slot_row, 1.0, 0.0).astype(BF16)
    tc_ref[...] = _dot(to_slots, t).astype(BF16)
    c_hi, c_lo = _split2(comb)
    packed = jnp.where(lane < N_EXPERTS, c_hi, pltpu.roll(c_lo.astype(F32), N_EXPERTS, 1).astype(BF16))
    moved = _dot(to_slots, packed)
    cw_ref[...] = moved + pltpu.roll(moved, LANES - N_EXPERTS, 1)
    oc_ref[...] = jnp.zeros(oc_ref.shape, F32)

    for blk in range(tm // MOE_ROWS):
        rows = slice(blk * MOE_ROWS, (blk + 1) * MOE_ROWS)
        for g in range(MOE_GROUPS):
            @pl.when((starts[g] < (blk + 1) * MOE_ROWS) & (starts[g] + sizes[g] > blk * MOE_ROWS))
            def _():
                oc_ref[rows, :] += expert_block(tc_ref[rows, :], cw_ref[rows, :], g)

    from_slots = jnp.where(lax.broadcasted_iota(jnp.int32, (tm, tm), 1).astype(F32) == slot, 1.0, 0.0).astype(BF16)
    moe = _dot(from_slots, oc_ref[...].astype(BF16))
    out_ref[0] = _layer_norm(alpha * x1_ref[0] + gf_ref[0] * moe, g2_ref[...], b2_ref[...])


def _moe(t, comb, x1, gf, wg, wu, wd, g2, b2, *, alpha, tm):
    b, l, d = x1.shape
    row_map = lambda i, r: (i, r, 0)
    const2 = lambda i, r: (0, 0)
    const3 = lambda i, r: (0, 0, 0)
    resident = pl.Buffered(1)
    return pl.pallas_call(
        functools.partial(_moe_kernel, alpha=alpha),
        grid=(b, l // tm),
        in_specs=[pl.BlockSpec((1, tm, d), row_map),
                  pl.BlockSpec((1, tm, LANES), row_map),
                  pl.BlockSpec((1, tm, d), row_map),
                  pl.BlockSpec((1, 1, d), lambda i, r: (i, 0, 0)),
                  pl.BlockSpec(wg.shape, const3, pipeline_mode=resident),
                  pl.BlockSpec(wu.shape, const3, pipeline_mode=resident),
                  pl.BlockSpec(wd.shape, const3, pipeline_mode=resident),
                  pl.BlockSpec((1, d), const2),
                  pl.BlockSpec((1, d), const2)],
        out_specs=pl.BlockSpec((1, tm, d), row_map),
        out_shape=jax.ShapeDtypeStruct((b, l, d), F32),
        scratch_shapes=[pltpu.VMEM((tm, d), BF16),
                        pltpu.VMEM((tm, LANES), F32),
                        pltpu.VMEM((tm, d), F32)],
        compiler_params=_params("parallel", "parallel"),
        name="moe",
    )(t, comb, x1, gf, wg, wu, wd, g2, b2)


def _rope_tables(n_rows):
    rows = jnp.broadcast_to(jnp.arange(n_rows, dtype=F32)[:, None], (n_rows, GRID_W)).reshape(-1)
    cols = jnp.broadcast_to(jnp.arange(GRID_W, dtype=F32)[None, :], (n_rows, GRID_W)).reshape(-1)
    axis_dim = HEAD_DIM // 2
    inv_freq = jnp.power(ROPE_THETA, -jnp.arange(0, axis_dim, 2, dtype=F32) / axis_dim)
    ang = jnp.concatenate([rows[:, None] * inv_freq, cols[:, None] * inv_freq], -1)
    cos, sin = jnp.cos(ang), jnp.sin(ang)
    cos_h = jnp.concatenate([cos, cos], -1)
    sin_h = jnp.concatenate([-sin, sin], -1)
    reps = LANES // HEAD_DIM
    return jnp.tile(cos_h, (1, reps)), jnp.tile(sin_h, (1, reps))


def _head_perm(n_heads):
    within = jnp.concatenate([jnp.arange(0, HEAD_DIM, 2), jnp.arange(1, HEAD_DIM, 2)])
    return (jnp.arange(n_heads)[:, None] * HEAD_DIM + within[None, :]).reshape(-1)


def kernel(x, c, ctx, c_ctx, w_mod, b_mod, w_in, conv_w, conv_b, dt_bias, a_log, d_skip, ssd_norm_g, q_norm_g,
           k_norm_g, w_out, ln1_g, ln1_b, w_rg, b_rg, w_re, b_re, w_gate, w_up, w_down, ln2_g, ln2_b):
    depth = w_in.shape[0]
    assert depth == 1, "single-layer block only"
    b, l, d = x.shape
    s_ctx = ctx.shape[1]
    assert d == D_MODEL and l % CHUNK == 0 and s_ctx % CHUNK == 0 and l % GRID_W == 0
    alpha = (2.0 * depth) ** 0.25

    m_rows = -(-(b + 1) // SUBLANES) * SUBLANES
    c_rows = jnp.zeros((m_rows, d), F32).at[:b].set(c).at[b].set(c_ctx)
    mod = _modulation(c_rows, w_mod[0], b_mod[0])
    sh_a, sc_a, g_a, sh_f, sc_f, g_f = [mod[:b, i * d:(i + 1) * d].reshape(b, 1, d) for i in range(6)]
    sh_c = mod[b:b + 1, 0:d].reshape(1, 1, d)
    sc_c = mod[b:b + 1, d:2 * d].reshape(1, 1, d)

    w = w_in[0]
    perm_q, perm_k = _head_perm(ATTN_Q_HEADS), _head_perm(ATTN_KV_HEADS)
    w_dt = jnp.pad(w[:, OFF_DT:OFF_K], ((0, 0), (0, DT_PAD - 2 * SSD_HEADS)))
    w_full = jnp.concatenate([w[:, :OFF_Q], w[:, OFF_Q:OFF_XBC][:, perm_q], w[:, OFF_XBC:OFF_DT],
                              w[:, OFF_K:OFF_V][:, perm_k], w[:, OFF_V:], w_dt], axis=1).astype(BF16)
    w_side = w_full[:, OFF_XBC:]
    within = perm_k[:HEAD_DIM]
    gq = jnp.tile(q_norm_g[0][within], ATTN_Q_HEADS).reshape(1, D_ATTN) * (HEAD_DIM ** -0.5 * math.log2(math.e))
    gk = jnp.tile(k_norm_g[0][within], ATTN_KV_HEADS).reshape(1, D_KV)
    lane = jnp.arange(MXU_DIM)
    ones_bd = (lane[:, None] // HEAD_DIM == lane[None, :] // HEAD_DIM).astype(BF16)
    cos, sin = _rope_tables(l // GRID_W)

    convw, convb = conv_w[0], conv_b[0].reshape(1, D_XBC)
    tm_in = 512 if l % 512 == 0 else CHUNK
    act_c, k_c, v_c, dt_c = _in_proj(ctx, sc_c, sh_c, w_side, ones_bd, gq, gk, convw, convb, None, None,
                                     with_zq=False, tm=s_ctx)
    z, q, act, k, v, dt = _in_proj(x, sc_a, sh_a, w_full, ones_bd, gq, gk, convw, convb, cos, sin,
                                   with_zq=True, tm=tm_in)

    pad_dt = lambda u: jnp.pad(u.reshape(1, 2 * SSD_HEADS).astype(F32), ((0, 0), (0, DT_PAD - 2 * SSD_HEADS)))
    dtb = pad_dt(dt_bias[0])
    a_neg = pad_dt(-jnp.exp(a_log[0].astype(F32)))
    skip = jnp.repeat(d_skip[0].astype(F32), SSD_HEAD_DIM).reshape(1, D_SSD)
    ng = ssd_norm_g[0].reshape(1, D_SSD)
    h_ctx = _ssd_ctx(act_c, dt_c, dtb, a_neg)
    y_ssd = _ssd(act, dt, z, h_ctx, dtb, a_neg, skip, ng)

    o = _attention(q, k_c, k, v_c, v, tq=512 if l % 512 == 0 else ATTN_ROWS)

    wr = jnp.concatenate([w_re[0], jnp.repeat(w_rg[0], EXPERTS_PER_GROUP, axis=1)], axis=1)
    wr = jnp.pad(wr, ((0, 0), (0, LANES - 2 * N_EXPERTS)))
    wr_hi = wr.astype(BF16)
    wr = jnp.concatenate([wr_hi, (wr - wr_hi.astype(F32)).astype(BF16)], axis=1)
    br = jnp.concatenate([b_re[0], jnp.repeat(b_rg[0], EXPERTS_PER_GROUP)])
    br = jnp.pad(br, (0, LANES - 2 * N_EXPERTS)).reshape(1, LANES)
    x1, t, comb = _out_proj(y_ssd, o, x, g_a, sc_f, sh_f, w_out[0].astype(BF16), ln1_g[0].reshape(1, d),
                            ln1_b[0].reshape(1, d), wr, br, alpha=alpha, tm=tm_in)

    wd = w_down[0].astype(BF16).reshape(MOE_GROUPS, EXPERTS_PER_GROUP * D_EXPERT, d)
    return _moe(t, comb, x1, g_f, w_gate[0].astype(BF16), w_up[0].astype(BF16), wd, ln2_g[0].reshape(1, d),
                ln2_b[0].reshape(1, d), alpha=alpha, tm=tm_in)
```
